```python
import jax, jax.numpy as jnp
from jax import lax
import numpy as np

D_MODEL = 1024
BATCH = 4
SEQ = 4096
DEPTH = 4
DEC_BATCH = 32
DEC_SEQ = 4
PAST_LEN = 8192
PAGE_SIZE = 128

N_MIXERS = 3
N_A = (DEPTH + 2) // 3
N_B = (DEPTH + 1) // 3
N_C = DEPTH // 3

A_HEADS = 8
A_HEAD_DIM = D_MODEL // A_HEADS
MOBA_BLOCK = 256
MOBA_TOPK = 3
MOBA_Q_CHUNK = 64

LRU_WIDTH = D_MODEL
LRU_BLOCKS = 4
LRU_BLOCK_W = LRU_WIDTH // LRU_BLOCKS
CONV_W = 4
LRU_C = 8.0

R_HEADS = 4
R_KDIM = D_MODEL // R_HEADS
R_VDIM = 2 * D_MODEL // R_HEADS
R_CHUNK = 128
ROPE_BASE = 10000.0

N_EXPERTS = 32
TOP_K = 4
D_FF = D_MODEL
SWIGLU_LIMIT = 7.0
SWIGLU_ALPHA = 1.702
MOE_BLOCK = 256

DN_ALPHA = (2 * DEPTH) ** 0.25
DN_BETA = (8 * DEPTH) ** -0.25
LN_EPS = 1e-5

kernel_name = 'moba_rglru_retention_moe_decoder_step'


def layer_norm(x, g, b):
    xf = x.astype(jnp.float32)
    mu = xf.mean(-1, keepdims=True)
    var = jnp.square(xf - mu).mean(-1, keepdims=True)
    return ((xf - mu) * lax.rsqrt(var + LN_EPS)).astype(x.dtype) * g + b


def rope(x, pos):
    half = x.shape[-1] // 2
    inv = ROPE_BASE ** (-jnp.arange(half, dtype=jnp.float32) / half)
    ang = pos.astype(jnp.float32)[:, None] * inv[None, :]
    cos, sin = jnp.cos(ang)[:, None, :], jnp.sin(ang)[:, None, :]
    x1 = x[..., :half].astype(jnp.float32)
    x2 = x[..., half:].astype(jnp.float32)
    return jnp.concatenate([x1 * cos - x2 * sin, x2 * cos + x1 * sin], axis=-1).astype(x.dtype)


def to_blocks(k):
    B, L, H, hd = k.shape
    nb = -(-L // MOBA_BLOCK)
    k = jnp.pad(k, ((0, 0), (0, nb * MOBA_BLOCK - L), (0, 0), (0, 0)))
    return k.reshape(B, nb, MOBA_BLOCK, H, hd).transpose(0, 3, 1, 2, 4)


def moba_attend(q, qpos, kb, vb):
    H, Q, _ = q.shape
    nb = kb.shape[1]
    k_eff = min(MOBA_TOPK, nb)
    kmean = kb.astype(jnp.float32).mean(axis=2)
    gate = jnp.einsum('hqd,hnd->hqn', q.astype(jnp.float32), kmean)
    own = qpos // MOBA_BLOCK
    fully_past = jnp.arange(nb)[None, :] < own[:, None]
    gate = jnp.where(fully_past[None], gate, -jnp.inf)
    _, sel = lax.top_k(gate, k_eff)
    sel_ok = jnp.arange(k_eff)[None, :] < own[:, None]
    blocks = jnp.concatenate([sel, jnp.broadcast_to(own[None, :, None], (H, Q, 1)).astype(sel.dtype)], axis=-1)
    kg = jax.vmap(lambda kh, bh: kh[bh])(kb, blocks)
    vg = jax.vmap(lambda vh, bh: vh[bh])(vb, blocks)
    kpos = blocks[..., None] * MOBA_BLOCK + jnp.arange(MOBA_BLOCK)
    blk_ok = jnp.concatenate([jnp.broadcast_to(sel_ok[None], (H, Q, k_eff)), jnp.ones((H, Q, 1), bool)], axis=-1)
    ok = blk_ok[..., None] & (kpos <= qpos[None, :, None, None])
    s = jnp.einsum('hqd,hqkpd->hqkp', q, kg).astype(jnp.float32) * (A_HEAD_DIM ** -0.5)
    s = jnp.where(ok, s, -jnp.inf)
    p = jax.nn.softmax(s.reshape(H, Q, -1), axis=-1).reshape(s.shape)
    return jnp.einsum('hqkp,hqkpd->hqd', p.astype(vg.dtype), vg)


def moba_prompt(q, k, v):
    B, S = q.shape[:2]
    kb, vb = to_blocks(k), to_blocks(v)
    qt = q.transpose(0, 2, 1, 3)
    qc = min(MOBA_Q_CHUNK, S)
    nq = S // qc

    def one(idx):
        bi, ci = idx // nq, idx % nq
        qi = lax.dynamic_slice_in_dim(qt[bi], ci * qc, qc, axis=1)
        qpos = ci * qc + jnp.arange(qc)
        return moba_attend(qi, qpos, kb[bi], vb[bi])

    o = lax.map(one, jnp.arange(B * nq))
    return o.reshape(B, nq, A_HEADS, qc, A_HEAD_DIM).transpose(0, 1, 3, 2, 4).reshape(B, S, A_HEADS, A_HEAD_DIM)


def moba_sample(q, k, v, k_past, v_past):
    P, T = k_past.shape[1], q.shape[1]
    kb = to_blocks(jnp.concatenate([k_past.astype(k.dtype), k], axis=1))
    vb = to_blocks(jnp.concatenate([v_past.astype(v.dtype), v], axis=1))
    qpos = P + jnp.arange(T)
    o = lax.map(lambda a: moba_attend(a[0], qpos, a[1], a[2]), (q.transpose(0, 2, 1, 3), kb, vb))
    return o.transpose(0, 2, 1, 3)


def moba_mixer(u, w_qkv, w_o, k_past, v_past):
    B, T, _ = u.shape
    qkv = (u @ w_qkv).reshape(B, T, 3, A_HEADS, A_HEAD_DIM)
    q, k, v = qkv[:, :, 0], qkv[:, :, 1], qkv[:, :, 2]
    if k_past is None:
        o = moba_prompt(q, k, v)
    else:
        o = moba_sample(q, k, v, k_past, v_past)
    return o.reshape(B, T, D_MODEL) @ w_o, k, v


def linear_scan(a, b, h0):
    b = b.at[:, 0].add(a[:, 0] * h0)

    def comb(l, r):
        return l[0] * r[0], r[0] * l[1] + r[1]

    _, h = lax.associative_scan(comb, (a, b), axis=1)
    return h


def rglru_mixer(u, w_in, b_in, conv_w, conv_b, w_ra, b_ra, w_ix, b_ix, lam, w_out, conv0, h0):
    B, T, _ = u.shape
    f32 = jnp.float32
    br = u @ w_in + b_in
    y_branch = jax.nn.gelu(br[..., :LRU_WIDTH])
    xr = br[..., LRU_WIDTH:]
    if conv0 is None:
        conv0 = jnp.zeros((B, CONV_W - 1, LRU_WIDTH), xr.dtype)
    xpad = jnp.concatenate([conv0.astype(xr.dtype), xr], axis=1)
    xc = conv_b
    for i in range(CONV_W):
        xc = xc + xpad[:, i:i + T] * conv_w[i]
    xb = xc.reshape(B, T, LRU_BLOCKS, LRU_BLOCK_W)
    r = jax.nn.sigmoid((jnp.einsum('btnw,nwv->btnv', xb, w_ra).reshape(B, T, LRU_WIDTH) + b_ra).astype(f32))
    ig = jax.nn.sigmoid((jnp.einsum('btnw,nwv->btnv', xb, w_ix).reshape(B, T, LRU_WIDTH) + b_ix).astype(f32))
    log_a = -LRU_C * r * jax.nn.softplus(-lam.astype(f32))
    a = jnp.exp(log_a)
    b = jnp.sqrt(-jnp.expm1(2.0 * log_a)) * (ig * xc.astype(f32))
    if h0 is None:
        h0 = jnp.zeros((B, LRU_WIDTH), f32)
    h = linear_scan(a, b, h0.astype(f32))
    y = (h.astype(u.dtype) * y_branch) @ w_out
    return y, xpad[:, T:], h[:, -1]


def retention_mixer(u, pos0, w_in, gn_g, gn_b, w_o, s0):
    B, T, _ = u.shape
    f32 = jnp.float32
    HK, HV = R_HEADS * R_KDIM, R_HEADS * R_VDIM
    proj = u @ w_in
    q = proj[..., :HK].reshape(B, T, R_HEADS, R_KDIM)
    k = proj[..., HK:2 * HK].reshape(B, T, R_HEADS, R_KDIM) * (R_KDIM ** -0.5)
    v = proj[..., 2 * HK:2 * HK + HV].reshape(B, T, R_HEADS, R_VDIM)
    g = proj[..., 2 * HK + HV:]
    pos = pos0 + jnp.arange(T)
    q, k = rope(q, pos), rope(k, pos)
    C = R_CHUNK if T % R_CHUNK == 0 else T
    nc = T // C

    def chunks(a):
        return a.astype(f32).reshape(B, nc, C, R_HEADS, a.shape[-1]).transpose(1, 0, 3, 2, 4)

    lg = jnp.log1p(-jnp.exp2(-5.0 - jnp.arange(R_HEADS, dtype=f32)))
    idx = jnp.arange(C, dtype=f32)
    diff = idx[:, None] - idx[None, :]
    dmask = jnp.where(diff >= 0, jnp.exp(jnp.maximum(diff, 0.0)[None] * lg[:, None, None]), 0.0)
    cross_decay = jnp.exp((idx + 1.0)[None, :] * lg[:, None])[..., None]
    k_decay = jnp.exp((C - 1.0 - idx)[None, :] * lg[:, None])[..., None]
    chunk_decay = jnp.exp(C * lg)[:, None, None]

    def step(S, qkv):
        qc, kc, vc = qkv
        inner = jnp.einsum('bhnd,bhmd->bhnm', qc, kc) * dmask
        o = jnp.einsum('bhnm,bhme->bhne', inner, vc) + jnp.einsum('bhnd,bhde->bhne', qc, S) * cross_decay
        S = chunk_decay * S + jnp.einsum('bhmd,bhme->bhde', kc * k_decay, vc)
        return S, o

    if s0 is None:
        s0 = jnp.zeros((B, R_HEADS, R_KDIM, R_VDIM), f32)
    s_fin, o = lax.scan(step, s0.astype(f32), (chunks(q), chunks(k), chunks(v)))
    o = o.transpose(1, 0, 3, 2, 4).reshape(B, T, R_HEADS, R_VDIM)
    mu = o.mean(-1, keepdims=True)
    var = jnp.square(o - mu).mean(-1, keepdims=True)
    o = ((o - mu) * lax.rsqrt(var + LN_EPS)).reshape(B, T, HV).astype(u.dtype) * gn_g + gn_b
    return (jax.nn.silu(g) * o) @ w_o, s_fin


def moe_ffn(u, w_r, b_r, w1, b1, w2, b2):
    B, T, D = u.shape
    n = B * T
    nk = n * TOP_K
    x = u.reshape(n, D)
    logits = (x @ w_r + b_r).astype(jnp.float32)
    top_v, top_e = lax.top_k(logits, TOP_K)
    gates = jax.nn.softmax(top_v, axis=-1).reshape(nk)
    e_flat = top_e.reshape(nk)
    tok = jnp.repeat(jnp.arange(n, dtype=jnp.int32), TOP_K)
    order = jnp.argsort(e_flat)
    e_s, tok_s, g_s = e_flat[order], tok[order], gates[order]
    counts = jnp.zeros((N_EXPERTS,), jnp.int32).at[e_flat].add(1)
    padded = (counts + MOE_BLOCK - 1) // MOE_BLOCK * MOE_BLOCK
    p_end = jnp.cumsum(padded)
    dest = (p_end - padded)[e_s] + jnp.arange(nk, dtype=jnp.int32) - (jnp.cumsum(counts) - counts)[e_s]
    n_blk = -(-nk // MOE_BLOCK) + N_EXPERTS
    xbuf = jnp.zeros((n_blk * MOE_BLOCK, D), x.dtype).at[dest].set(x[tok_s])
    blk_e = jnp.minimum(jnp.searchsorted(p_end, jnp.arange(n_blk, dtype=jnp.int32) * MOE_BLOCK, side='right'), N_EXPERTS - 1)

    def expert_block(args):
        xb, e = args
        h = xb @ w1[e] + b1[e]
        gt = jnp.minimum(h[:, :D_FF], SWIGLU_LIMIT)
        up = jnp.clip(h[:, D_FF:], -SWIGLU_LIMIT, SWIGLU_LIMIT)
        return ((up + 1.0) * (gt * jax.nn.sigmoid(SWIGLU_ALPHA * gt))) @ w2[e] + b2[e]

    ybuf = lax.map(expert_block, (xbuf.reshape(n_blk, MOE_BLOCK, D), blk_e)).reshape(n_blk * MOE_BLOCK, D)
    y = jnp.zeros((n, D), jnp.float32).at[tok_s].add(ybuf[dest].astype(jnp.float32) * g_s[:, None])
    return y.astype(u.dtype).reshape(B, T, D)


def setup_inputs(seed: int = 0) -> dict:
    key = jax.random.key(seed)
    keys = iter(jax.random.split(key, 64))
    D = D_MODEL

    def nrm(shape, scale=1.0):
        return jax.random.normal(next(keys), shape, jnp.float32) * scale

    n_pages = PAST_LEN // PAGE_SIZE
    n_used = DEC_BATCH * n_pages
    n_pool = n_used + n_used // 4
    page_table = jax.random.permutation(next(keys), n_pool)[:n_used].reshape(DEC_BATCH, n_pages).astype(jnp.int32)
    a0 = jax.random.uniform(next(keys), (N_B, LRU_WIDTH), jnp.float32, 0.9, 0.999)
    HK, HV = R_HEADS * R_KDIM, R_HEADS * R_VDIM
    return {
        'x_prompt': nrm((BATCH, SEQ, D)),
        'x_sample': nrm((DEC_BATCH, DEC_SEQ, D)),
        'cache_k': nrm((n_pool, N_A, PAGE_SIZE, A_HEADS, A_HEAD_DIM)),
        'cache_v': nrm((n_pool, N_A, PAGE_SIZE, A_HEADS, A_HEAD_DIM)),
        'state_lru_conv': nrm((DEC_BATCH, N_B, CONV_W - 1, LRU_WIDTH)),
        'state_lru_h': nrm((DEC_BATCH, N_B, LRU_WIDTH), 0.5),
        'state_ret': nrm((DEC_BATCH, N_C, R_HEADS, R_KDIM, R_VDIM), 0.5),
        'page_table': page_table,
        'c_prompt': nrm((BATCH, D)),
        'c_sample': nrm((DEC_BATCH, D)),
        'w_ada': nrm((DEPTH, D, 6 * D), 0.5 * D ** -0.5),
        'b_ada': nrm((DEPTH, 6 * D), 0.02),
        'ln_g': 1.0 + nrm((DEPTH, 2, D), 0.02),
        'ln_b': nrm((DEPTH, 2, D), 0.02),
        'a_w_qkv': nrm((N_A, D, 3 * D), D ** -0.5),
        'a_w_o': nrm((N_A, D, D), D ** -0.5 * DN_BETA),
        'lru_w_in': nrm((N_B, D, 2 * LRU_WIDTH), D ** -0.5),
        'lru_b_in': nrm((N_B, 2 * LRU_WIDTH), 0.02),
        'lru_conv_w': nrm((N_B, CONV_W, LRU_WIDTH), CONV_W ** -0.5),
        'lru_conv_b': nrm((N_B, LRU_WIDTH), 0.02),
        'lru_w_ra': nrm((N_B, LRU_BLOCKS, LRU_BLOCK_W, LRU_BLOCK_W), LRU_BLOCK_W ** -0.5),
        'lru_b_ra': nrm((N_B, LRU_WIDTH), 0.02),
        'lru_w_ix': nrm((N_B, LRU_BLOCKS, LRU_BLOCK_W, LRU_BLOCK_W), LRU_BLOCK_W ** -0.5),
        'lru_b_ix': nrm((N_B, LRU_WIDTH), 0.02),
        'lru_lambda': jnp.log(a0) - jnp.log1p(-a0),
        'lru_w_out': nrm((N_B, LRU_WIDTH, D), LRU_WIDTH ** -0.5 * DN_BETA),
        'ret_w_in': nrm((N_C, D, 2 * HK + 2 * HV), D ** -0.5),
        'ret_gn_g': 1.0 + nrm((N_C, HV), 0.02),
        'ret_gn_b': nrm((N_C, HV), 0.02),
        'ret_w_o': nrm((N_C, HV, D), HV ** -0.5 * DN_BETA),
        'moe_w_router': nrm((DEPTH, D, N_EXPERTS), D ** -0.5),
        'moe_b_router': nrm((DEPTH, N_EXPERTS), 0.01),
        'moe_w1': nrm((DEPTH, N_EXPERTS, D, 2 * D_FF), D ** -0.5),
        'moe_b1': nrm((DEPTH, N_EXPERTS, 2 * D_FF), 0.02),
        'moe_w2': nrm((DEPTH, N_EXPERTS, D_FF, D), D_FF ** -0.5 * DN_BETA),
        'moe_b2': nrm((DEPTH, N_EXPERTS, D), 0.02),
    }


def reference(x_prompt, x_sample, cache_k, cache_v, state_lru_conv, state_lru_h, state_ret, page_table,
              c_prompt, c_sample, w_ada, b_ada, ln_g, ln_b, a_w_qkv, a_w_o,
              lru_w_in, lru_b_in, lru_conv_w, lru_conv_b, lru_w_ra, lru_b_ra, lru_w_ix, lru_b_ix,
              lru_lambda, lru_w_out, ret_w_in, ret_gn_g, ret_gn_b, ret_w_o,
              moe_w_router, moe_b_router, moe_w1, moe_b1, moe_w2, moe_b2):
    dec_b, n_pages = page_table.shape
    past_len = n_pages * cache_k.shape[2]

    def past_kv(j):
        kp = cache_k[page_table, j].reshape(dec_b, past_len, A_HEADS, A_HEAD_DIM)
        vp = cache_v[page_table, j].reshape(dec_b, past_len, A_HEADS, A_HEAD_DIM)
        return kp, vp

    def run(x, c, pos0, sample):
        mod = jnp.einsum('bd,lde->lbe', jax.nn.silu(c), w_ada) + b_ada[:, None, :]
        ks, vs, convs, hs, ss = [], [], [], [], []
        for layer in range(DEPTH):
            kind, j = layer % N_MIXERS, layer // N_MIXERS
            sh_m, sc_m, g_m, sh_f, sc_f, g_f = jnp.split(mod[layer][:, None, :], 6, axis=-1)
            u = x * (1.0 + sc_m) + sh_m
            if kind == 0:
                kp, vp = past_kv(j) if sample else (None, None)
                y, k, v = moba_mixer(u, a_w_qkv[j], a_w_o[j], kp, vp)
                ks.append(k)
                vs.append(v)
            elif kind == 1:
                conv0 = state_lru_conv[:, j] if sample else None
                h0 = state_lru_h[:, j] if sample else None
                y, cs, h = rglru_mixer(u, lru_w_in[j], lru_b_in[j], lru_conv_w[j], lru_conv_b[j],
                                       lru_w_ra[j], lru_b_ra[j], lru_w_ix[j], lru_b_ix[j],
                                       lru_lambda[j], lru_w_out[j], conv0, h0)
                convs.append(cs)
                hs.append(h)
            else:
                s0 = state_ret[:, j] if sample else None
                y, s = retention_mixer(u, pos0, ret_w_in[j], ret_gn_g[j], ret_gn_b[j], ret_w_o[j], s0)
                ss.append(s)
            x = layer_norm(DN_ALPHA * x + g_m * y, ln_g[layer, 0], ln_b[layer, 0])
            u = x * (1.0 + sc_f) + sh_f
            y = moe_ffn(u, moe_w_router[layer], moe_b_router[layer], moe_w1[layer], moe_b1[layer],
                        moe_w2[layer], moe_b2[layer])
            x = layer_norm(DN_ALPHA * x + g_f * y, ln_g[layer, 1], ln_b[layer, 1])
        return x, jnp.stack(ks, 1), jnp.stack(vs, 1), jnp.stack(convs, 1), jnp.stack(hs, 1), jnp.stack(ss, 1)

    y_prompt, k_p, v_p, conv_p, h_p, s_p = run(x_prompt, c_prompt, 0, False)
    y_sample, k_s, v_s, conv_s, h_s, s_s = run(x_sample, c_sample, past_len, True)
    return (y_prompt, y_sample, k_p, v_p, conv_p, h_p, s_p, k_s, v_s, conv_s, h_s, s_s)
```

```python
import functools

import jax
import jax.numpy as jnp
from jax import lax
from jax.experimental import pallas as pl
from jax.experimental.pallas import tpu as pltpu

F32 = jnp.float32
BF16 = jnp.bfloat16
HIGHEST = lax.Precision.HIGHEST
NEG_INF = float("-inf")

MOBA_BLOCK = 256
MOBA_TOPK = 3
MOE_TOPK = 4
LRU_C = 8.0
ROPE_BASE = 10000.0
SWIGLU_LIMIT = 7.0
SWIGLU_ALPHA = 1.702
LN_EPS = 1e-5

MOE_TILE = 256
ROW_TILE = 256
V7X_VMEM_BYTES = 64 * 2**20
VMEM_LIMIT = V7X_VMEM_BYTES - 8 * 2**20


def _params(*sem):
    return pltpu.CompilerParams(dimension_semantics=sem, vmem_limit_bytes=VMEM_LIMIT)


def _nt_dot(a, b, **kw):
    return lax.dot_general(a, b, (((1,), (1,)), ((), ())), preferred_element_type=F32, **kw)


def _mm_kernel(*refs, has_bias, pre_silu):
    if has_bias:
        x_ref, w_ref, b_ref, o_ref, wbf_ref = refs
    else:
        x_ref, w_ref, o_ref, wbf_ref = refs

    @pl.when(pl.program_id(1) == 0)
    def _():
        wbf_ref[...] = w_ref[...].astype(BF16)

    x = x_ref[...]
    if pre_silu:
        x = x * jax.nn.sigmoid(x)
    acc = jnp.dot(x.astype(BF16), wbf_ref[...], preferred_element_type=F32)
    if has_bias:
        acc = acc + b_ref[...]
    o_ref[...] = acc


def matmul(x, w, b=None, *, pre_silu=False, tm=512, tn=512):
    M, K = x.shape
    N = w.shape[1]
    tm, tn = min(tm, M), min(tn, N)
    assert M % tm == 0 and N % tn == 0, (M, N, tm, tn)
    in_specs = [pl.BlockSpec((tm, K), lambda j, i: (i, 0)),
                pl.BlockSpec((K, tn), lambda j, i: (0, j))]
    args = [x, w]
    if b is not None:
        in_specs.append(pl.BlockSpec((1, tn), lambda j, i: (0, j)))
        args.append(b.reshape(1, N))
    return pl.pallas_call(
        functools.partial(_mm_kernel, has_bias=b is not None, pre_silu=pre_silu),
        out_shape=jax.ShapeDtypeStruct((M, N), F32),
        grid=(N // tn, M // tm),
        in_specs=in_specs,
        out_specs=pl.BlockSpec((tm, tn), lambda j, i: (i, j)),
        scratch_shapes=[pltpu.VMEM((K, tn), BF16)],
        compiler_params=_params("arbitrary", "arbitrary"),
    )(*args)


class ModSource:
    def __init__(self, mod, rows_per_seq, tm, d):
        self.d = d
        self.tm = tm
        if rows_per_seq % tm == 0:
            self.per_seq = True
            self.tiles_per_seq = rows_per_seq // tm
            self.array = mod.reshape(mod.shape[0], 1, mod.shape[1])
        else:
            self.per_seq = False
            self.array = jnp.repeat(mod, rows_per_seq, axis=0)

    def spec(self, chunk):
        if self.per_seq:
            tps = self.tiles_per_seq
            return pl.BlockSpec((None, 1, self.d), lambda i, *_: (i // tps, 0, chunk))
        return pl.BlockSpec((self.tm, self.d), lambda i, *_: (i, chunk))


def _layer_norm(z, g, b):
    mu = jnp.mean(z, axis=-1, keepdims=True)
    d = z - mu
    var = jnp.mean(d * d, axis=-1, keepdims=True)
    return d * lax.rsqrt(var + LN_EPS) * g + b


def _router(u, wr, br, e_ref, gt_ref):
    logits = jnp.dot(u, wr, preferred_element_type=F32, precision=HIGHEST) + br
    n_e = logits.shape[-1]
    lane = lax.broadcasted_iota(jnp.int32, logits.shape, 1).astype(F32)
    slot = lax.broadcasted_iota(jnp.int32, (logits.shape[0], MOE_TOPK), 1)
    idx_out = jnp.zeros((logits.shape[0], MOE_TOPK), F32)
    val_out = jnp.zeros((logits.shape[0], MOE_TOPK), F32)
    cur = logits
    top = None
    for k in range(MOE_TOPK):
        m = jnp.max(cur, axis=-1, keepdims=True)
        idx = jnp.min(jnp.where(cur == m, lane, float(n_e)), axis=-1, keepdims=True)
        if top is None:
            top = m
        idx_out = jnp.where(slot == k, idx, idx_out)
        val_out = jnp.where(slot == k, m, val_out)
        cur = jnp.where(lane == idx, NEG_INF, cur)
    ex = jnp.exp(val_out - top)
    gt_ref[...] = ex / jnp.sum(ex, axis=-1, keepdims=True)
    e_ref[...] = idx_out.astype(jnp.int32)


def _mod_kernel(x_ref, sc_ref, sh_ref, u_ref):
    u_ref[...] = x_ref[...] * (1.0 + sc_ref[...]) + sh_ref[...]


def modulate(x, ms, sc_chunk, sh_chunk):
    n, d = x.shape
    tm = ms.tm
    row = pl.BlockSpec((tm, d), lambda i: (i, 0))
    return pl.pallas_call(
        _mod_kernel,
        out_shape=jax.ShapeDtypeStruct((n, d), F32),
        grid=(n // tm,),
        in_specs=[row, ms.spec(sc_chunk), ms.spec(sh_chunk)],
        out_specs=row,
        compiler_params=_params("arbitrary"),
    )(x, ms.array, ms.array)


def _ln_router_kernel(x_ref, y_ref, g_ref, lng_ref, lnb_ref, sc_ref, sh_ref, wr_ref, br_ref,
                      xo_ref, uo_ref, e_ref, gt_ref, *, alpha):
    xn = _layer_norm(alpha * x_ref[...] + g_ref[...] * y_ref[...], lng_ref[...], lnb_ref[...])
    xo_ref[...] = xn
    u = xn * (1.0 + sc_ref[...]) + sh_ref[...]
    uo_ref[...] = u
    _router(u, wr_ref[...], br_ref[...], e_ref, gt_ref)


def ln_router(x, y, ms, g_chunk, sc_chunk, sh_chunk, ln_g, ln_b, w_r, b_r, alpha):
    n, d = x.shape
    tm = ms.tm
    n_e = w_r.shape[1]
    row = pl.BlockSpec((tm, d), lambda i: (i, 0))
    vec = pl.BlockSpec((1, d), lambda i: (0, 0))
    kk = pl.BlockSpec((tm, MOE_TOPK), lambda i: (i, 0))
    return pl.pallas_call(
        functools.partial(_ln_router_kernel, alpha=alpha),
        out_shape=(jax.ShapeDtypeStruct((n, d), F32), jax.ShapeDtypeStruct((n, d), F32),
                   jax.ShapeDtypeStruct((n, MOE_TOPK), jnp.int32), jax.ShapeDtypeStruct((n, MOE_TOPK), F32)),
        grid=(n // tm,),
        in_specs=[row, row, ms.spec(g_chunk), vec, vec, ms.spec(sc_chunk), ms.spec(sh_chunk),
                  pl.BlockSpec((d, n_e), lambda i: (0, 0)), pl.BlockSpec((1, n_e), lambda i: (0, 0))],
        out_specs=(row, row, kk, kk),
        compiler_params=_params("arbitrary"),
    )(x, y, ms.array, ln_g.reshape(1, d), ln_b.reshape(1, d), ms.array, ms.array, w_r, b_r.reshape(1, n_e))


def _ln_combine_kernel(pos_ref, x_ref, gt_ref, ybuf_ref, g_ref, lng_ref, lnb_ref, *rest, alpha, do_mod, tm):
    if do_mod:
        sc_ref, sh_ref, xo_ref, uo_ref, buf, sem = rest
    else:
        xo_ref, buf, sem = rest
    base = pl.program_id(0) * (tm * MOE_TOPK)
    for k in range(MOE_TOPK):
        def issue(t, carry, k=k):
            row = pos_ref[base + k * tm + t]
            pltpu.make_async_copy(ybuf_ref.at[pl.ds(row, 1)], buf.at[k, pl.ds(t, 1)], sem).start()
            return carry
        lax.fori_loop(0, tm, issue, 0)
    for k in range(MOE_TOPK):
        pltpu.make_async_copy(ybuf_ref.at[pl.ds(0, tm)], buf.at[k], sem).wait()
    gt = gt_ref[...]
    y = gt[:, 0:1] * buf[0]
    for k in range(1, MOE_TOPK):
        y = y + gt[:, k:k + 1] * buf[k]
    xn = _layer_norm(alpha * x_ref[...] + g_ref[...] * y, lng_ref[...], lnb_ref[...])
    xo_ref[...] = xn
    if do_mod:
        uo_ref[...] = xn * (1.0 + sc_ref[...]) + sh_ref[...]


def ln_combine(x, gates, pos_tiles, ybuf, ms, g_chunk, ln_g, ln_b, alpha, next_ms=None, sc_chunk=None, sh_chunk=None):
    n, d = x.shape
    tm = ms.tm
    do_mod = next_ms is not None
    row = pl.BlockSpec((tm, d), lambda i, p: (i, 0))
    vec = pl.BlockSpec((1, d), lambda i, p: (0, 0))
    in_specs = [row, pl.BlockSpec((tm, MOE_TOPK), lambda i, p: (i, 0)), pl.BlockSpec(memory_space=pl.ANY),
                ms.spec(g_chunk), vec, vec]
    args = [x, gates, ybuf, ms.array, ln_g.reshape(1, d), ln_b.reshape(1, d)]
    out_shape = [jax.ShapeDtypeStruct((n, d), F32)]
    out_specs = [row]
    if do_mod:
        in_specs += [next_ms.spec(sc_chunk), next_ms.spec(sh_chunk)]
        args += [next_ms.array, next_ms.array]
        out_shape.append(jax.ShapeDtypeStruct((n, d), F32))
        out_specs.append(row)
    res = pl.pallas_call(
        functools.partial(_ln_combine_kernel, alpha=alpha, do_mod=do_mod, tm=tm),
        out_shape=tuple(out_shape),
        grid_spec=pltpu.PrefetchScalarGridSpec(
            num_scalar_prefetch=1, grid=(n // tm,), in_specs=in_specs, out_specs=tuple(out_specs),
            scratch_shapes=[pltpu.VMEM((MOE_TOPK, tm, d), F32), pltpu.SemaphoreType.DMA]),
        compiler_params=_params("arbitrary"),
    )(pos_tiles, *args)
    return res if do_mod else (res[0], None)


def _gather_kernel(idx_ref, src_ref, o_ref, sem, *, tm):
    base = pl.program_id(0) * tm

    def issue(r, carry):
        pltpu.make_async_copy(src_ref.at[pl.ds(idx_ref[base + r], 1)], o_ref.at[pl.ds(r, 1)], sem).start()
        return carry

    lax.fori_loop(0, tm, issue, 0)
    pltpu.make_async_copy(src_ref.at[pl.ds(0, tm)], o_ref, sem).wait()


def gather_rows(src, idx, tm):
    m = idx.shape[0]
    d = src.shape[1]
    assert m % tm == 0 and src.shape[0] >= tm
    return pl.pallas_call(
        functools.partial(_gather_kernel, tm=tm),
        out_shape=jax.ShapeDtypeStruct((m, d), src.dtype),
        grid_spec=pltpu.PrefetchScalarGridSpec(
            num_scalar_prefetch=1, grid=(m // tm,),
            in_specs=[pl.BlockSpec(memory_space=pl.ANY)],
            out_specs=pl.BlockSpec((tm, d), lambda i, idx: (i, 0)),
            scratch_shapes=[pltpu.SemaphoreType.DMA]),
        compiler_params=_params("arbitrary"),
    )(idx, src)


def _experts_kernel(meta_ref, x_ref, w1_ref, b1_ref, w2_ref, b2_ref, o_ref, w1bf, w2bf, *, d_ff, tile, n_blk):
    i = pl.program_id(0)
    e = meta_ref[i]
    prev = meta_ref[jnp.maximum(i - 1, 0)]
    active = i * tile < meta_ref[n_blk]

    @pl.when(active & ((i == 0) | (e != prev)))
    def _():
        w1bf[...] = w1_ref[...].astype(BF16)
        w2bf[...] = w2_ref[...].astype(BF16)

    @pl.when(active)
    def _():
        h = jnp.dot(x_ref[...].astype(BF16), w1bf[...], preferred_element_type=F32) + b1_ref[...]
        gt = jnp.minimum(h[:, :d_ff], SWIGLU_LIMIT)
        up = jnp.clip(h[:, d_ff:], -SWIGLU_LIMIT, SWIGLU_LIMIT)
        act = (up + 1.0) * (gt * jax.nn.sigmoid(SWIGLU_ALPHA * gt))
        o_ref[...] = jnp.dot(act.astype(BF16), w2bf[...], preferred_element_type=F32) + b2_ref[...]

    @pl.when(jnp.logical_not(active))
    def _():
        o_ref[...] = jnp.zeros_like(o_ref)


def experts(xs, meta, w1, b1, w2, b2, tile):
    rows, d = xs.shape
    n_e, _, ff2 = w1.shape
    d_ff = ff2 // 2
    n_blk = rows // tile
    return pl.pallas_call(
        functools.partial(_experts_kernel, d_ff=d_ff, tile=tile, n_blk=n_blk),
        out_shape=jax.ShapeDtypeStruct((rows, d), F32),
        grid_spec=pltpu.PrefetchScalarGridSpec(
            num_scalar_prefetch=1, grid=(n_blk,),
            in_specs=[pl.BlockSpec((tile, d), lambda i, m: (i, 0)),
                      pl.BlockSpec((None, d, ff2), lambda i, m: (m[i], 0, 0)),
                      pl.BlockSpec((None, 1, ff2), lambda i, m: (m[i], 0, 0)),
                      pl.BlockSpec((None, d_ff, d), lambda i, m: (m[i], 0, 0)),
                      pl.BlockSpec((None, 1, d), lambda i, m: (m[i], 0, 0))],
            out_specs=pl.BlockSpec((tile, d), lambda i, m: (i, 0)),
            scratch_shapes=[pltpu.VMEM((d, ff2), BF16), pltpu.VMEM((d_ff, d), BF16)]),
        compiler_params=_params("arbitrary"),
    )(meta, xs, w1, b1.reshape(n_e, 1, ff2), w2, b2.reshape(n_e, 1, d))


def route_tables(top_e, n_e, tile, tm):
    n = top_e.shape[0]
    nk = n * MOE_TOPK
    e_flat = top_e.reshape(nk)
    onehot = (e_flat[:, None] == jnp.arange(n_e, dtype=jnp.int32)[None, :]).astype(jnp.int32)
    csum = jnp.cumsum(onehot, axis=0)
    rank = jnp.sum(onehot * (csum - 1), axis=1)
    counts = csum[-1]
    padded = (counts + tile - 1) // tile * tile
    p_end = jnp.cumsum(padded)
    pos = (p_end - padded)[e_flat] + rank
    n_blk = -(-nk // tile) + n_e
    tok_sorted = jnp.zeros((n_blk * tile,), jnp.int32).at[pos].set(jnp.arange(nk, dtype=jnp.int32) // MOE_TOPK)
    blk_e = jnp.minimum(jnp.searchsorted(p_end, jnp.arange(n_blk, dtype=jnp.int32) * tile, side='right'), n_e - 1)
    meta = jnp.concatenate([blk_e.astype(jnp.int32), p_end[-1:].astype(jnp.int32)])
    pos_tiles = pos.reshape(n // tm, tm, MOE_TOPK).transpose(0, 2, 1).reshape(nk)
    return tok_sorted, meta, pos_tiles.astype(jnp.int32)


def _block_rank_select(gate, valid, col, n_blocks):
    gate = jnp.where(valid, gate, NEG_INF)
    cnt = jnp.zeros(gate.shape, F32)
    for n2 in range(n_blocks):
        g2 = gate[:, n2:n2 + 1]
        ahead = (g2 > gate) | ((g2 == gate) & (col > n2))
        cnt = cnt + jnp.where(ahead, 1.0, 0.0)
    return valid & (cnt < float(MOBA_TOPK))


def _moba_prompt_kernel(q_ref, k_ref, v_ref, o_ref, km_ref, *, n_blocks, blk, scale):
    i = pl.program_id(2)

    @pl.when(i == 0)
    def _():
        for n in range(n_blocks):
            km_ref[n:n + 1, :] = jnp.mean(k_ref[n * blk:(n + 1) * blk, :], axis=0, keepdims=True)

    q = q_ref[...]
    gate = _nt_dot(q, km_ref[...], precision=HIGHEST)
    col = lax.broadcasted_iota(jnp.int32, gate.shape, 1)
    sel = _block_rank_select(gate, col < i, col, n_blocks)
    sel_f = jnp.where(sel, 1.0, 0.0)

    qb = q.astype(BF16)
    start = pl.multiple_of(i * blk, blk)
    s = _nt_dot(qb, k_ref[pl.ds(start, blk), :].astype(BF16)) * scale
    r_id = lax.broadcasted_iota(jnp.int32, s.shape, 0)
    c_id = lax.broadcasted_iota(jnp.int32, s.shape, 1)
    s = jnp.where(c_id <= r_id, s, NEG_INF)
    m = jnp.max(s, axis=-1, keepdims=True)
    p = jnp.exp(s - m)
    l = jnp.sum(p, axis=-1, keepdims=True)
    acc = jnp.dot(p.astype(BF16), v_ref[pl.ds(start, blk), :].astype(BF16), preferred_element_type=F32)

    def body(n, carry):
        m, l, acc = carry
        st = pl.multiple_of(n * blk, blk)
        s = _nt_dot(qb, k_ref[pl.ds(st, blk), :].astype(BF16)) * scale
        chosen = jnp.sum(jnp.where(col == n, sel_f, 0.0), axis=-1, keepdims=True) > 0.0
        s = jnp.where(chosen, s, NEG_INF)
        m_new = jnp.maximum(m, jnp.max(s, axis=-1, keepdims=True))
        a = jnp.exp(m - m_new)
        p = jnp.exp(s - m_new)
        l = a * l + jnp.sum(p, axis=-1, keepdims=True)
        acc = a * acc + jnp.dot(p.astype(BF16), v_ref[pl.ds(st, blk), :].astype(BF16), preferred_element_type=F32)
        return m_new, l, acc

    m, l, acc = lax.fori_loop(0, i, body, (m, l, acc))
    o_ref[...] = acc / l


def moba_prompt(qkv, n_seq, seq, heads, hd):
    blk = MOBA_BLOCK
    assert seq % blk == 0
    nq = seq // blk
    return pl.pallas_call(
        functools.partial(_moba_prompt_kernel, n_blocks=nq, blk=blk, scale=hd ** -0.5),
        out_shape=jax.ShapeDtypeStruct((n_seq * seq, heads * hd), F32),
        grid=(n_seq, heads, nq),
        in_specs=[pl.BlockSpec((blk, hd), lambda b, h, i: (b * nq + i, h)),
                  pl.BlockSpec((seq, hd), lambda b, h, i: (b, heads + h)),
                  pl.BlockSpec((seq, hd), lambda b, h, i: (b, 2 * heads + h))],
        out_specs=pl.BlockSpec((blk, hd), lambda b, h, i: (b * nq + i, h)),
        scratch_shapes=[pltpu.VMEM((nq, hd), F32)],
        compiler_params=_params("arbitrary", "arbitrary", "arbitrary"),
    )(qkv, qkv, qkv)


PAGES_PER_STEP = 4


def _moba_sample_kernel(pt_ref, qbd_ref, kn_ref, vn_ref, *rest, n_pages, page, n_past_blocks, t_new, scale):
    k_refs = rest[:PAGES_PER_STEP]
    v_refs = rest[PAGES_PER_STEP:2 * PAGES_PER_STEP]
    o_ref, sc_ref, gate_ref, acc_ref, l_ref = rest[2 * PAGES_PER_STEP:]
    ph = pl.program_id(1)
    s = pl.program_id(2)
    pages_per_block = MOBA_BLOCK // page
    qbd = qbd_ref[...]
    qbd16 = qbd.astype(BF16)
    R = qbd.shape[0]
    lane = lax.broadcasted_iota(jnp.int32, (R, 128), 1)

    @pl.when((ph == 0) & (s == 0))
    def _():
        gate_ref[...] = jnp.zeros_like(gate_ref)

    @pl.when(ph == 0)
    def _():
        for r in range(PAGES_PER_STEP):
            kp = k_refs[r][...]
            sc_ref[s * PAGES_PER_STEP + r] = _nt_dot(qbd16, kp.astype(BF16)) * scale
            ksum = jnp.sum(kp, axis=0, keepdims=True) * (1.0 / MOBA_BLOCK)
            g = jnp.sum(qbd * ksum, axis=-1, keepdims=True)
            blk_id = (s * PAGES_PER_STEP + r) // pages_per_block
            gate_ref[...] += jnp.where(lane == blk_id, g, 0.0)

    @pl.when((ph == 1) & (s == 0))
    def _():
        gate = gate_ref[...]
        sel = _block_rank_select(gate, lane < n_past_blocks, lane, n_past_blocks)
        s_own = _nt_dot(qbd16, kn_ref[...].astype(BF16)) * scale
        tp = s_own.shape[1]
        r_t = lax.broadcasted_iota(jnp.int32, s_own.shape, 0) % t_new
        c_t = lax.broadcasted_iota(jnp.int32, s_own.shape, 1)
        s_own = jnp.where((c_t <= r_t) & (c_t < t_new), s_own, NEG_INF)
        m = jnp.max(s_own, axis=-1, keepdims=True)
        for pg in range(n_pages):
            b_id = pg // pages_per_block
            chosen = sel[:, b_id:b_id + 1]
            m = jnp.maximum(m, jnp.max(jnp.where(chosen, sc_ref[pg], NEG_INF), axis=-1, keepdims=True))
        p_own = jnp.exp(s_own - m)
        l = jnp.sum(p_own, axis=-1, keepdims=True)
        for pg in range(n_pages):
            b_id = pg // pages_per_block
            chosen = sel[:, b_id:b_id + 1]
            p = jnp.where(chosen, jnp.exp(sc_ref[pg] - m), 0.0)
            sc_ref[pg] = p
            l = l + jnp.sum(p, axis=-1, keepdims=True)
        l_ref[...] = jnp.broadcast_to(l, l_ref.shape)
        acc_ref[...] = jnp.dot(p_own.astype(BF16), vn_ref[...].astype(BF16), preferred_element_type=F32)

    @pl.when(ph == 1)
    def _():
        acc = acc_ref[...]
        for r in range(PAGES_PER_STEP):
            p = sc_ref[s * PAGES_PER_STEP + r]
            acc = acc + jnp.dot(p.astype(BF16), v_refs[r][...].astype(BF16), preferred_element_type=F32)
        acc_ref[...] = acc

    @pl.when((ph == 1) & (s == pl.num_programs(2) - 1))
    def _():
        o_ref[...] = acc_ref[...] / l_ref[:, 0:1]


def moba_sample(qkv, cache_k, cache_v, page_table, layer_j, n_seq, t_new, heads, hd):
    d = heads * hd
    n_pool, n_a, page = cache_k.shape[0], cache_k.shape[1], cache_k.shape[2]
    n_pages = page_table.shape[1]
    past = n_pages * page
    assert past % MOBA_BLOCK == 0 and MOBA_BLOCK % page == 0 and t_new <= MOBA_BLOCK
    assert n_pages % PAGES_PER_STEP == 0
    n_steps = n_pages // PAGES_PER_STEP
    n_past_blocks = past // MOBA_BLOCK
    assert MOBA_TOPK <= n_past_blocks <= 128
    R = heads * t_new
    tp = 8 * (-(-t_new // 8))
    qkv3 = qkv.reshape(n_seq, t_new, 3 * d)
    q = qkv3[:, :, :d]
    head_of_col = jnp.arange(d, dtype=jnp.int32) // hd
    head_of_row = jnp.arange(R, dtype=jnp.int32) // t_new
    qbd = jnp.where(head_of_row[None, :, None] == head_of_col[None, None, :], jnp.tile(q, (1, heads, 1)), 0.0)
    pad = ((0, 0), (0, tp - t_new), (0, 0))
    k_new = jnp.pad(qkv3[:, :, d:2 * d], pad)
    v_new = jnp.pad(qkv3[:, :, 2 * d:], pad)
    ck = cache_k.reshape(n_pool, n_a, page, d)
    cv = cache_v.reshape(n_pool, n_a, page, d)
    pt = page_table.reshape(-1).astype(jnp.int32)

    def k_map(r):
        return lambda b, ph, s, pt: (pt[b * n_pages + jnp.where(ph == 0, s, n_steps - 1) * PAGES_PER_STEP + r], layer_j, 0, 0)

    def v_map(r):
        return lambda b, ph, s, pt: (pt[b * n_pages + jnp.where(ph == 0, 0, s) * PAGES_PER_STEP + r], layer_j, 0, 0)

    seq_blk = lambda rows: pl.BlockSpec((None, rows, d), lambda b, ph, s, pt: (b, 0, 0))
    acc = pl.pallas_call(
        functools.partial(_moba_sample_kernel, n_pages=n_pages, page=page, n_past_blocks=n_past_blocks,
                          t_new=t_new, scale=hd ** -0.5),
        out_shape=jax.ShapeDtypeStruct((n_seq, R, d), F32),
        grid_spec=pltpu.PrefetchScalarGridSpec(
            num_scalar_prefetch=1, grid=(n_seq, 2, n_steps),
            in_specs=[seq_blk(R), seq_blk(tp), seq_blk(tp)]
                     + [pl.BlockSpec((None, None, page, d), k_map(r)) for r in range(PAGES_PER_STEP)]
                     + [pl.BlockSpec((None, None, page, d), v_map(r)) for r in range(PAGES_PER_STEP)],
            out_specs=seq_blk(R),
            scratch_shapes=[pltpu.VMEM((n_pages, R, page), F32), pltpu.VMEM((R, 128), F32),
                            pltpu.VMEM((R, d), F32), pltpu.VMEM((R, 128), F32)]),
        compiler_params=_params("arbitrary", "arbitrary", "arbitrary"),
    )(pt, qbd, k_new, v_new, *([ck] * PAGES_PER_STEP), *([cv] * PAGES_PER_STEP))
    acc = acc.reshape(n_seq, heads, t_new, heads, hd)
    o = jnp.einsum('bhthc->bthc', acc)
    return o.reshape(n_seq * t_new, d)


def _softplus(x):
    return jnp.maximum(x, 0.0) + jnp.log1p(jnp.exp(-jnp.abs(x)))


def _lru_gates(xc, wra_ref, bra, wix_ref, bix, sp, n, bw):
    cs = slice(n * bw, (n + 1) * bw)
    xcb = xc[:, cs]
    xcb16 = xcb.astype(BF16)
    r = jax.nn.sigmoid(jnp.dot(xcb16, wra_ref[n].astype(BF16), preferred_element_type=F32) + bra[:, cs])
    ig = jax.nn.sigmoid(jnp.dot(xcb16, wix_ref[n].astype(BF16), preferred_element_type=F32) + bix[:, cs])
    log_a = -LRU_C * r * sp[:, cs]
    a = jnp.exp(log_a)
    b = jnp.sqrt(jnp.tanh(-log_a) * (1.0 + a * a)) * (ig * xcb)
    return a, b


def _lru_prompt_kernel(yb_ref, xr_ref, cw_ref, cb_ref, wra_ref, bra_ref, wix_ref, bix_ref, lam_ref,
                       z_ref, cs_ref, hl_ref, xbuf, hcar, *, tc, n_lru_blocks, bw, conv_w):
    i = pl.program_id(1)

    @pl.when(i == 0)
    def _():
        xbuf[0:8, :] = jnp.zeros((8, xbuf.shape[1]), F32)
        hcar[...] = jnp.zeros_like(hcar)

    xr = xr_ref[...]
    xbuf[8:8 + tc, :] = xr
    cw = cw_ref[...]
    xc = cb_ref[...] + xr * cw[conv_w - 1:conv_w, :]
    for s in range(1, conv_w):
        xc = xc + xbuf[8 - s:8 - s + tc, :] * cw[conv_w - 1 - s:conv_w - s, :]
    xbuf[0:8, :] = xbuf[tc:tc + 8, :]
    sp = _softplus(-lam_ref[...])
    bra, bix = bra_ref[...], bix_ref[...]
    row = lax.broadcasted_iota(jnp.int32, (tc, bw), 0)
    for n in range(n_lru_blocks):
        cs = slice(n * bw, (n + 1) * bw)
        a, b = _lru_gates(xc, wra_ref, bra, wix_ref, bix, sp, n, bw)
        sh = 1
        while sh < tc:
            keep = row >= sh
            a_prev = jnp.where(keep, pltpu.roll(a, sh, 0), 1.0)
            b_prev = jnp.where(keep, pltpu.roll(b, sh, 0), 0.0)
            b = a * b_prev + b
            a = a * a_prev
            sh *= 2
        h = a * hcar[:, cs] + b
        hcar[:, cs] = h[tc - 1:tc, :]
        z_ref[:, cs] = h * jax.nn.gelu(yb_ref[:, cs])
    cs_ref[...] = xbuf[8 - (conv_w - 1):8, :]
    hl_ref[...] = hcar[...]


def lru_prompt(br, n_seq, seq, conv_w, conv_b, w_ra, b_ra, w_ix, b_ix, lam, tc=256):
    w = br.shape[1] // 2
    cw = conv_w.shape[0]
    nb, bw = w_ra.shape[0], w_ra.shape[1]
    assert seq % tc == 0 and cw - 1 <= 8 <= tc
    nt = seq // tc
    vec = pl.BlockSpec((1, w), lambda b, i: (0, 0))
    wblk = pl.BlockSpec((nb, bw, bw), lambda b, i: (0, 0, 0))
    return pl.pallas_call(
        functools.partial(_lru_prompt_kernel, tc=tc, n_lru_blocks=nb, bw=bw, conv_w=cw),
        out_shape=(jax.ShapeDtypeStruct((n_seq * seq, w), F32),
                   jax.ShapeDtypeStruct((n_seq, cw - 1, w), F32),
                   jax.ShapeDtypeStruct((n_seq, 1, w), F32)),
        grid=(n_seq, nt),
        in_specs=[pl.BlockSpec((tc, w), lambda b, i: (b * nt + i, 0)),
                  pl.BlockSpec((tc, w), lambda b, i: (b * nt + i, 1)),
                  pl.BlockSpec((cw, w), lambda b, i: (0, 0)), vec, wblk, vec, wblk, vec, vec],
        out_specs=(pl.BlockSpec((tc, w), lambda b, i: (b * nt + i, 0)),
                   pl.BlockSpec((None, cw - 1, w), lambda b, i: (b, 0, 0)),
                   pl.BlockSpec((None, 1, w), lambda b, i: (b, 0, 0))),
        scratch_shapes=[pltpu.VMEM((8 + tc, w), F32), pltpu.VMEM((1, w), F32)],
        compiler_params=_params("arbitrary", "arbitrary"),
    )(br, br, conv_w, conv_b.reshape(1, w), w_ra, b_ra.reshape(1, w), w_ix, b_ix.reshape(1, w), lam.reshape(1, w))


def _lru_sample_kernel(br_ref, c0_ref, h0_ref, cw_ref, cb_ref, wra_ref, bra_ref, wix_ref, bix_ref, lam_ref,
                       z_ref, cs_ref, hl_ref, *, t_new, n_lru_blocks, bw, conv_w, w):
    br = br_ref[...]
    yb, xr = br[:, :w], br[:, w:]
    xpad = jnp.concatenate([c0_ref[...], xr], axis=0)
    cw = cw_ref[...]
    xc = cb_ref[...] + xpad[0:t_new, :] * cw[0:1, :]
    for i in range(1, conv_w):
        xc = xc + xpad[i:i + t_new, :] * cw[i:i + 1, :]
    sp = _softplus(-lam_ref[...])
    bra, bix = bra_ref[...], bix_ref[...]
    gelu_y = jax.nn.gelu(yb)
    for n in range(n_lru_blocks):
        cs = slice(n * bw, (n + 1) * bw)
        a, b = _lru_gates(xc, wra_ref, bra, wix_ref, bix, sp, n, bw)
        h = h0_ref[:, cs]
        for t in range(t_new):
            h = a[t:t + 1, :] * h + b[t:t + 1, :]
            z_ref[t:t + 1, cs] = h * gelu_y[t:t + 1, cs]
        hl_ref[:, cs] = h
    cs_ref[...] = xpad[t_new:t_new + conv_w - 1, :]


def lru_sample(br, conv0, h0, n_seq, t_new, conv_w, conv_b, w_ra, b_ra, w_ix, b_ix, lam):
    w = br.shape[1] // 2
    cw = conv_w.shape[0]
    nb, bw = w_ra.shape[0], w_ra.shape[1]
    vec = pl.BlockSpec((1, w), lambda b: (0, 0))
    wblk = pl.BlockSpec((nb, bw, bw), lambda b: (0, 0, 0))
    per_seq = lambda rows, cols: pl.BlockSpec((None, rows, cols), lambda b: (b, 0, 0))
    z, cs, hl = pl.pallas_call(
        functools.partial(_lru_sample_kernel, t_new=t_new, n_lru_blocks=nb, bw=bw, conv_w=cw, w=w),
        out_shape=(jax.ShapeDtypeStruct((n_seq, t_new, w), F32),
                   jax.ShapeDtypeStruct((n_seq, cw - 1, w), F32),
                   jax.ShapeDtypeStruct((n_seq, 1, w), F32)),
        grid=(n_seq,),
        in_specs=[per_seq(t_new, 2 * w), per_seq(cw - 1, w), per_seq(1, w),
                  pl.BlockSpec((cw, w), lambda b: (0, 0)), vec, wblk, vec, wblk, vec, vec],
        out_specs=(per_seq(t_new, w), per_seq(cw - 1, w), per_seq(1, w)),
        compiler_params=_params("arbitrary"),
    )(br.reshape(n_seq, t_new, 2 * w), conv0, h0.reshape(n_seq, 1, w), conv_w, conv_b.reshape(1, w),
      w_ra, b_ra.reshape(1, w), w_ix, b_ix.reshape(1, w), lam.reshape(1, w))
    return z.reshape(n_seq * t_new, w), cs, hl


def _rope(x, cos, sin, half):
    x1, x2 = x[:, :half], x[:, half:]
    return jnp.concatenate([x1 * cos - x2 * sin, x2 * cos + x1 * sin], axis=-1)


def _group_norm_gate(o, g, gn_g, gn_b):
    mu = jnp.mean(o, axis=-1, keepdims=True)
    d = o - mu
    var = jnp.mean(d * d, axis=-1, keepdims=True)
    on = d * lax.rsqrt(var + LN_EPS) * gn_g + gn_b
    return (g * jax.nn.sigmoid(g)) * on


def _ret_prompt_kernel(q_ref, k_ref, v_ref, g_ref, cos_ref, sin_ref, dm_ref, cd_ref, kd_ref, chd_ref,
                       gng_ref, gnb_ref, z_ref, s_ref, S, *, kscale, half):
    c = pl.program_id(2)

    @pl.when(c == 0)
    def _():
        S[...] = jnp.zeros_like(S)

    cos, sin = cos_ref[...], sin_ref[...]
    q = _rope(q_ref[...], cos, sin, half)
    k = _rope(k_ref[...] * kscale, cos, sin, half)
    qb, kb, vb = q.astype(BF16), k.astype(BF16), v_ref[...].astype(BF16)
    inner = _nt_dot(qb, kb) * dm_ref[...]
    s_old = S[...]
    o = (jnp.dot(inner.astype(BF16), vb, preferred_element_type=F32)
         + jnp.dot(qb, s_old.astype(BF16), preferred_element_type=F32) * cd_ref[...])
    kdt = (k * kd_ref[...]).T.astype(BF16)
    S[...] = chd_ref[...] * s_old + jnp.dot(kdt, vb, preferred_element_type=F32)
    z_ref[...] = _group_norm_gate(o, g_ref[...], gng_ref[...], gnb_ref[...])

    @pl.when(c == pl.num_programs(2) - 1)
    def _():
        s_ref[...] = S[...]


def _decay_tables(heads, c):
    lg = jnp.log1p(-jnp.exp2(-5.0 - jnp.arange(heads, dtype=F32)))
    idx = jnp.arange(c, dtype=F32)
    diff = idx[:, None] - idx[None, :]
    dmask = jnp.where(diff >= 0, jnp.exp(jnp.maximum(diff, 0.0)[None] * lg[:, None, None]), 0.0)
    cross = jnp.exp((idx + 1.0)[None, :] * lg[:, None])[..., None]
    kdec = jnp.exp((c - 1.0 - idx)[None, :] * lg[:, None])[..., None]
    chunk = jnp.exp(c * lg)[:, None, None]
    return dmask, cross, kdec, chunk


def _rope_tables(pos, half):
    inv = ROPE_BASE ** (-jnp.arange(half, dtype=F32) / half)
    ang = pos.astype(F32)[:, None] * inv[None, :]
    return jnp.cos(ang), jnp.sin(ang)


def ret_prompt(proj, n_seq, seq, heads, kdim, vdim, gn_g, gn_b, tc=256):
    assert seq % tc == 0 and vdim % kdim == 0
    nc = seq // tc
    hk, hv = heads * kdim, heads * vdim
    half = kdim // 2
    cos, sin = _rope_tables(jnp.arange(seq), half)
    dmask, cross, kdec, chunk = _decay_tables(heads, tc)
    v0, g0 = 2 * hk // vdim, (2 * hk + hv) // vdim
    row = lambda b, h, c: b * nc + c
    per_head = lambda r, cdim: pl.BlockSpec((None, r, cdim), lambda b, h, c: (h, 0, 0))
    return pl.pallas_call(
        functools.partial(_ret_prompt_kernel, kscale=kdim ** -0.5, half=half),
        out_shape=(jax.ShapeDtypeStruct((n_seq * seq, hv), F32),
                   jax.ShapeDtypeStruct((n_seq, heads, kdim, vdim), F32)),
        grid=(n_seq, heads, nc),
        in_specs=[pl.BlockSpec((tc, kdim), lambda b, h, c: (row(b, h, c), h)),
                  pl.BlockSpec((tc, kdim), lambda b, h, c: (row(b, h, c), heads + h)),
                  pl.BlockSpec((tc, vdim), lambda b, h, c: (row(b, h, c), v0 + h)),
                  pl.BlockSpec((tc, vdim), lambda b, h, c: (row(b, h, c), g0 + h)),
                  pl.BlockSpec((tc, half), lambda b, h, c: (c, 0)),
                  pl.BlockSpec((tc, half), lambda b, h, c: (c, 0)),
                  per_head(tc, tc), per_head(tc, 1), per_head(tc, 1), per_head(1, 1),
                  pl.BlockSpec((1, vdim), lambda b, h, c: (0, h)),
                  pl.BlockSpec((1, vdim), lambda b, h, c: (0, h))],
        out_specs=(pl.BlockSpec((tc, vdim), lambda b, h, c: (row(b, h, c), h)),
                   pl.BlockSpec((None, None, kdim, vdim), lambda b, h, c: (b, h, 0, 0))),
        scratch_shapes=[pltpu.VMEM((kdim, vdim), F32)],
        compiler_params=_params("arbitrary", "arbitrary", "arbitrary"),
    )(proj, proj, proj, proj, cos, sin, dmask, cross, kdec, chunk, gn_g.reshape(1, hv), gn_b.reshape(1, hv))


def _ret_sample_kernel(p_ref, kt_ref, s0_ref, cos_ref, sin_ref, cost_ref, sint_ref, dm_ref, cd_ref, kd_ref, chd_ref,
                       gng_ref, gnb_ref, z_ref, s_ref, *, heads, kdim, vdim, t_new, kscale):
    half = kdim // 2
    hk, hv = heads * kdim, heads * vdim
    p = p_ref[...]
    cos, sin = cos_ref[...], sin_ref[...]
    cost, sint = cost_ref[...], sint_ref[...]
    for h in range(heads):
        q = _rope(p[:, h * kdim:(h + 1) * kdim], cos, sin, half)
        k = _rope(p[:, hk + h * kdim:hk + (h + 1) * kdim] * kscale, cos, sin, half)
        v = p[:, 2 * hk + h * vdim:2 * hk + (h + 1) * vdim]
        g = p[:, 2 * hk + hv + h * vdim:2 * hk + hv + (h + 1) * vdim]
        kt = kt_ref[h] * kscale
        k1, k2 = kt[:half, :], kt[half:, :]
        kt = jnp.concatenate([k1 * cost - k2 * sint, k2 * cost + k1 * sint], axis=0) * kd_ref[h]
        s0 = s0_ref[h]
        dm = dm_ref[h]
        o = jnp.dot(q, s0, preferred_element_type=F32, precision=HIGHEST) * cd_ref[h]
        s_new = chd_ref[h] * s0
        for m in range(t_new):
            inner_m = jnp.sum(q * k[m:m + 1, :], axis=-1, keepdims=True) * dm[:, m:m + 1]
            o = o + inner_m * v[m:m + 1, :]
            s_new = s_new + kt[:, m:m + 1] * v[m:m + 1, :]
        s_ref[h] = s_new
        z_ref[:, h * vdim:(h + 1) * vdim] = _group_norm_gate(
            o, g, gng_ref[:, h * vdim:(h + 1) * vdim], gnb_ref[:, h * vdim:(h + 1) * vdim])


def ret_sample(proj, s0, pos0, n_seq, t_new, heads, kdim, vdim, gn_g, gn_b):
    hk, hv = heads * kdim, heads * vdim
    half = kdim // 2
    width = 2 * hk + 2 * hv
    cos, sin = _rope_tables(pos0 + jnp.arange(t_new), half)
    dmask, cross, kdec, chunk = _decay_tables(heads, t_new)
    proj3 = proj.reshape(n_seq, t_new, width)
    kt = proj3[:, :, hk:2 * hk].reshape(n_seq, t_new, heads, kdim).transpose(0, 2, 3, 1)
    full = lambda *shape: pl.BlockSpec(shape, lambda b: (0,) * len(shape))
    z, s = pl.pallas_call(
        functools.partial(_ret_sample_kernel, heads=heads, kdim=kdim, vdim=vdim, t_new=t_new, kscale=kdim ** -0.5),
        out_shape=(jax.ShapeDtypeStruct((n_seq, t_new, hv), F32),
                   jax.ShapeDtypeStruct((n_seq, heads, kdim, vdim), F32)),
        grid=(n_seq,),
        in_specs=[pl.BlockSpec((None, t_new, width), lambda b: (b, 0, 0)),
                  pl.BlockSpec((None, heads, kdim, t_new), lambda b: (b, 0, 0, 0)),
                  pl.BlockSpec((None, heads, kdim, vdim), lambda b: (b, 0, 0, 0)),
                  full(t_new, half), full(t_new, half), full(half, t_new), full(half, t_new),
                  full(heads, t_new, t_new), full(heads, t_new, 1), full(heads, 1, t_new), full(heads, 1, 1),
                  full(1, hv), full(1, hv)],
        out_specs=(pl.BlockSpec((None, t_new, hv), lambda b: (b, 0, 0)),
                   pl.BlockSpec((None, heads, kdim, vdim), lambda b: (b, 0, 0, 0))),
        compiler_params=_params("arbitrary"),
    )(proj3, kt, s0, cos, sin, cos.T, sin.T, dmask, cross, kdec.transpose(0, 2, 1), chunk,
      gn_g.reshape(1, hv), gn_b.reshape(1, hv))
    return z.reshape(n_seq * t_new, hv), s


def kernel(x_prompt, x_sample, cache_k, cache_v, state_lru_conv, state_lru_h, state_ret, page_table, c_prompt, c_sample, w_ada, b_ada, ln_g, ln_b, a_w_qkv, a_w_o, lru_w_in, lru_b_in, lru_conv_w, lru_conv_b, lru_w_ra, lru_b_ra, lru_w_ix, lru_b_ix, lru_lambda, lru_w_out, ret_w_in, ret_gn_g, ret_gn_b, ret_w_o, moe_w_router, moe_b_router, moe_w1, moe_b1, moe_w2, moe_b2):
    depth, d = w_ada.shape[0], w_ada.shape[1]
    n_mixers = 3
    alpha = (2 * depth) ** 0.25
    heads, hd = cache_k.shape[3], cache_k.shape[4]
    r_heads, r_kdim, r_vdim = state_ret.shape[2], state_ret.shape[3], state_ret.shape[4]
    n_e = moe_w_router.shape[2]
    past_len = page_table.shape[1] * cache_k.shape[2]

    bp, bs = c_prompt.shape[0], c_sample.shape[0]
    n_c = bp + bs
    c_rows = 16 * (-(-n_c // 16))
    c_all = jnp.pad(jnp.concatenate([c_prompt, c_sample], axis=0), ((0, c_rows - n_c), (0, 0)))
    mods = [matmul(c_all, w_ada[l], b_ada[l], pre_silu=True, tn=1024) for l in range(depth)]

    def run(x, seq_lo, sample):
        n_seq, t, _ = x.shape
        n = n_seq * t
        tm = min(ROW_TILE, n)
        ms = [ModSource(mods[l][seq_lo:seq_lo + n_seq], t, tm, d) for l in range(depth)]
        x2 = x.reshape(n, d)
        u = modulate(x2, ms[0], 1, 0)
        ks, vs, convs, hs, ss = [], [], [], [], []
        for layer in range(depth):
            kind, j = layer % n_mixers, layer // n_mixers
            if kind == 0:
                qkv = matmul(u, a_w_qkv[j])
                if sample:
                    o = moba_sample(qkv, cache_k, cache_v, page_table, j, n_seq, t, heads, hd)
                else:
                    o = moba_prompt(qkv, n_seq, t, heads, hd)
                y = matmul(o, a_w_o[j])
                ks.append(qkv[:, d:2 * d].reshape(n_seq, t, heads, hd))
                vs.append(qkv[:, 2 * d:].reshape(n_seq, t, heads, hd))
            elif kind == 1:
                br = matmul(u, lru_w_in[j], lru_b_in[j])
                lru_args = (lru_conv_w[j], lru_conv_b[j], lru_w_ra[j], lru_b_ra[j], lru_w_ix[j], lru_b_ix[j],
                            lru_lambda[j])
                if sample:
                    z, cs, hl = lru_sample(br, state_lru_conv[:, j], state_lru_h[:, j], n_seq, t, *lru_args)
                else:
                    z, cs, hl = lru_prompt(br, n_seq, t, *lru_args)
                y = matmul(z, lru_w_out[j])
                convs.append(cs)
                hs.append(hl.reshape(n_seq, -1))
            else:
                proj = matmul(u, ret_w_in[j])
                if sample:
                    z, s_fin = ret_sample(proj, state_ret[:, j], past_len, n_seq, t, r_heads, r_kdim, r_vdim,
                                          ret_gn_g[j], ret_gn_b[j])
                else:
                    z, s_fin = ret_prompt(proj, n_seq, t, r_heads, r_kdim, r_vdim, ret_gn_g[j], ret_gn_b[j])
                y = matmul(z, ret_w_o[j])
                ss.append(s_fin)
            x2, u, top_e, gates = ln_router(x2, y, ms[layer], 2, 4, 3, ln_g[layer, 0], ln_b[layer, 0],
                                            moe_w_router[layer], moe_b_router[layer], alpha)
            tile = min(MOE_TILE, n)
            tok_sorted, meta, pos_tiles = route_tables(top_e, n_e, tile, tm)
            xs = gather_rows(u, tok_sorted, tile)
            ybuf = experts(xs, meta, moe_w1[layer], moe_b1[layer], moe_w2[layer], moe_b2[layer], tile)
            if layer + 1 < depth:
                x2, u = ln_combine(x2, gates, pos_tiles, ybuf, ms[layer], 5, ln_g[layer, 1], ln_b[layer, 1], alpha,
                                   ms[layer + 1], 1, 0)
            else:
                x2, u = ln_combine(x2, gates, pos_tiles, ybuf, ms[layer], 5, ln_g[layer, 1], ln_b[layer, 1], alpha)
        return (x2.reshape(n_seq, t, d), jnp.stack(ks, 1), jnp.stack(vs, 1), jnp.stack(convs, 1), jnp.stack(hs, 1),
                jnp.stack(ss, 1))

    y_p, k_p, v_p, conv_p, h_p, s_p = run(x_prompt, 0, False)
    y_s, k_s, v_s, conv_s, h_s, s_s = run(x_sample, bp, True)
    return (y_p, y_s, k_p, v_p, conv_p, h_p, s_p, k_s, v_s, conv_s, h_s, s_s)
```

```python
import functools

import jax
import jax.numpy as jnp
from jax import lax
from jax.experimental import pallas as pl
from jax.experimental.pallas import tpu as pltpu

F32 = jnp.float32
BF16 = jnp.bfloat16
HIGHEST = lax.Precision.HIGHEST
NEG_INF = float("-inf")

MOBA_BLOCK = 256
MOBA_TOPK = 3
MOE_TOPK = 4
LRU_C = 8.0
ROPE_BASE = 10000.0
SWIGLU_LIMIT = 7.0
SWIGLU_ALPHA = 1.702
LN_EPS = 1e-5

LANES = 128
DMA_UNROLL = 8
MOE_TILE = 256
ROW_TILE = 256
V7X_VMEM_BYTES = 64 * 2**20
VMEM_LIMIT = V7X_VMEM_BYTES - 8 * 2**20


def _params(*sem):
    return pltpu.CompilerParams(dimension_semantics=sem, vmem_limit_bytes=VMEM_LIMIT)


def _nt_dot(a, b, **kw):
    return lax.dot_general(a, b, (((1,), (1,)), ((), ())), preferred_element_type=F32, **kw)


def _mm_kernel(*refs, has_bias, pre_silu):
    if has_bias:
        x_ref, w_ref, b_ref, o_ref, wbf_ref = refs
    else:
        x_ref, w_ref, o_ref, wbf_ref = refs

    @pl.when(pl.program_id(1) == 0)
    def _():
        wbf_ref[...] = w_ref[...].astype(BF16)

    x = x_ref[...]
    if pre_silu:
        x = x * jax.nn.sigmoid(x)
    acc = jnp.dot(x.astype(BF16), wbf_ref[...], preferred_element_type=F32)
    if has_bias:
        acc = acc + b_ref[...]
    o_ref[...] = acc


def matmul(x, w, b=None, *, pre_silu=False, tm=512, tn=512):
    M, K = x.shape
    N = w.shape[1]
    tm, tn = min(tm, M), min(tn, N)
    assert M % tm == 0 and N % tn == 0, (M, N, tm, tn)
    in_specs = [pl.BlockSpec((tm, K), lambda j, i: (i, 0)),
                pl.BlockSpec((K, tn), lambda j, i: (0, j))]
    args = [x, w]
    if b is not None:
        in_specs.append(pl.BlockSpec((1, tn), lambda j, i: (0, j)))
        args.append(b.reshape(1, N))
    return pl.pallas_call(
        functools.partial(_mm_kernel, has_bias=b is not None, pre_silu=pre_silu),
        out_shape=jax.ShapeDtypeStruct((M, N), F32),
        grid=(N // tn, M // tm),
        in_specs=in_specs,
        out_specs=pl.BlockSpec((tm, tn), lambda j, i: (i, j)),
        scratch_shapes=[pltpu.VMEM((K, tn), BF16)],
        compiler_params=_params("arbitrary", "arbitrary"),
    )(*args)


def _qkv_kernel(x_ref, w_ref, qkv_ref, k3_ref, v3_ref, wbf_ref, *, d):
    @pl.when(pl.program_id(0) == 0)
    def _():
        wbf_ref[...] = w_ref[...].astype(BF16)

    acc = jnp.dot(x_ref[...].astype(BF16), wbf_ref[...], preferred_element_type=F32)
    qkv_ref[...] = acc
    k3_ref[...] = acc[:, d:2 * d].reshape(k3_ref.shape)
    v3_ref[...] = acc[:, 2 * d:].reshape(v3_ref.shape)


def qkv_project(u, w, heads, hd, tm=256):
    n, d = u.shape
    assert n % tm == 0 and w.shape == (d, 3 * d) and heads * hd == d
    leaf = jax.ShapeDtypeStruct((n, heads, hd), F32)
    leaf_blk = pl.BlockSpec((tm, heads, hd), lambda i: (i, 0, 0))
    return pl.pallas_call(
        functools.partial(_qkv_kernel, d=d),
        out_shape=(jax.ShapeDtypeStruct((n, 3 * d), F32), leaf, leaf),
        grid=(n // tm,),
        in_specs=[pl.BlockSpec((tm, d), lambda i: (i, 0)),
                  pl.BlockSpec((d, 3 * d), lambda i: (0, 0), pipeline_mode=pl.Buffered(1))],
        out_specs=(pl.BlockSpec((tm, 3 * d), lambda i: (i, 0)), leaf_blk, leaf_blk),
        scratch_shapes=[pltpu.VMEM((d, 3 * d), BF16)],
        compiler_params=_params("arbitrary"),
    )(u, w)


class ModSource:
    def __init__(self, mod, rows_per_seq, tm, d):
        self.d = d
        self.tm = tm
        if rows_per_seq % tm == 0:
            self.per_seq = True
            self.tiles_per_seq = rows_per_seq // tm
            self.array = mod.reshape(mod.shape[0], 1, mod.shape[1])
        else:
            self.per_seq = False
            self.array = jnp.repeat(mod, rows_per_seq, axis=0)

    def spec(self, chunk):
        if self.per_seq:
            tps = self.tiles_per_seq
            return pl.BlockSpec((None, 1, self.d), lambda i, *_: (i // tps, 0, chunk))
        return pl.BlockSpec((self.tm, self.d), lambda i, *_: (i, chunk))


def _layer_norm(z, g, b):
    mu = jnp.mean(z, axis=-1, keepdims=True)
    d = z - mu
    var = jnp.mean(d * d, axis=-1, keepdims=True)
    return d * lax.rsqrt(var + LN_EPS) * g + b


def _router(u, wr, br, e_ref, gt_ref, rk_ref, cnt_ref, carry):
    logits = jnp.dot(u, wr, preferred_element_type=F32, precision=HIGHEST) + br
    tm, n_e = logits.shape
    lane = lax.broadcasted_iota(jnp.int32, logits.shape, 1).astype(F32)
    slot = lax.broadcasted_iota(jnp.int32, (tm, MOE_TOPK), 1)
    idx_out = jnp.zeros((tm, MOE_TOPK), F32)
    val_out = jnp.zeros((tm, MOE_TOPK), F32)
    cur = logits
    top = None
    picks = []
    for k in range(MOE_TOPK):
        m = jnp.max(cur, axis=-1, keepdims=True)
        idx = jnp.min(jnp.where(cur == m, lane, float(n_e)), axis=-1, keepdims=True)
        if top is None:
            top = m
        idx_out = jnp.where(slot == k, idx, idx_out)
        val_out = jnp.where(slot == k, m, val_out)
        picks.append(lane == idx)
        cur = jnp.where(picks[-1], NEG_INF, cur)
    ex = jnp.exp(val_out - top)
    gt_ref[...] = ex / jnp.sum(ex, axis=-1, keepdims=True)
    e_ref[...] = idx_out.astype(jnp.int32)
    hot = sum(jnp.where(pk, 1.0, 0.0) for pk in picks)
    r_id = lax.broadcasted_iota(jnp.int32, (tm, tm), 0)
    c_id = lax.broadcasted_iota(jnp.int32, (tm, tm), 1)
    tri = jnp.where(c_id < r_id, 1.0, 0.0).astype(BF16)
    before = jnp.dot(tri, hot.astype(BF16), preferred_element_type=F32) + carry[...]
    rk_out = jnp.zeros((tm, MOE_TOPK), F32)
    for k in range(MOE_TOPK):
        rk = jnp.sum(jnp.where(picks[k], before, 0.0), axis=-1, keepdims=True)
        rk_out = jnp.where(slot == k, rk, rk_out)
    rk_ref[...] = rk_out.astype(jnp.int32)
    carry[...] += jnp.sum(hot, axis=0, keepdims=True)
    cnt_ref[...] = carry[...].astype(jnp.int32)


def _mod_kernel(x_ref, sc_ref, sh_ref, u_ref):
    u_ref[...] = x_ref[...] * (1.0 + sc_ref[...]) + sh_ref[...]


def modulate(x, ms, sc_chunk, sh_chunk):
    n, d = x.shape
    tm = ms.tm
    row = pl.BlockSpec((tm, d), lambda i: (i, 0))
    return pl.pallas_call(
        _mod_kernel,
        out_shape=jax.ShapeDtypeStruct((n, d), F32),
        grid=(n // tm,),
        in_specs=[row, ms.spec(sc_chunk), ms.spec(sh_chunk)],
        out_specs=row,
        compiler_params=_params("arbitrary"),
    )(x, ms.array, ms.array)


def _ln_router_kernel(x_ref, y_ref, cnt0_ref, g_ref, lng_ref, lnb_ref, sc_ref, sh_ref, wr_ref, br_ref,
                      xo_ref, u3_ref, e_ref, gt_ref, rk_ref, cnt_ref, carry, *, alpha):
    @pl.when(pl.program_id(0) == 0)
    def _():
        carry[...] = cnt0_ref[...].astype(F32)

    xn = _layer_norm(alpha * x_ref[...] + g_ref[...] * y_ref[...], lng_ref[...], lnb_ref[...])
    xo_ref[...] = xn
    u = xn * (1.0 + sc_ref[...]) + sh_ref[...]
    u3_ref[...] = u.reshape(u3_ref.shape)
    _router(u, wr_ref[...], br_ref[...], e_ref, gt_ref, rk_ref, cnt_ref, carry)


def ln_router(x, y, counts0, ms, g_chunk, sc_chunk, sh_chunk, ln_g, ln_b, w_r, b_r, alpha):
    n, d = x.shape
    tm = ms.tm
    n_e = w_r.shape[1]
    row = pl.BlockSpec((tm, d), lambda i: (i, 0))
    vec = pl.BlockSpec((1, d), lambda i: (0, 0))
    kk = pl.BlockSpec((tm, MOE_TOPK), lambda i: (i, 0))
    kk_i = jax.ShapeDtypeStruct((n, MOE_TOPK), jnp.int32)
    return pl.pallas_call(
        functools.partial(_ln_router_kernel, alpha=alpha),
        out_shape=(jax.ShapeDtypeStruct((n, d), F32), jax.ShapeDtypeStruct((n, d // LANES, LANES), F32),
                   kk_i, jax.ShapeDtypeStruct((n, MOE_TOPK), F32), kk_i,
                   jax.ShapeDtypeStruct((1, n_e), jnp.int32)),
        grid=(n // tm,),
        in_specs=[row, row, pl.BlockSpec((1, n_e), lambda i: (0, 0)), ms.spec(g_chunk), vec, vec,
                  ms.spec(sc_chunk), ms.spec(sh_chunk),
                  pl.BlockSpec((d, n_e), lambda i: (0, 0)), pl.BlockSpec((1, n_e), lambda i: (0, 0))],
        out_specs=(row, pl.BlockSpec((tm, d // LANES, LANES), lambda i: (i, 0, 0)), kk, kk, kk,
                   pl.BlockSpec((1, n_e), lambda i: (0, 0))),
        scratch_shapes=[pltpu.VMEM((1, n_e), F32)],
        compiler_params=_params("arbitrary"),
    )(x, y, counts0, ms.array, ln_g.reshape(1, d), ln_b.reshape(1, d), ms.array, ms.array, w_r,
      b_r.reshape(1, n_e))


def _ln_combine_kernel(pos_ref, x_ref, gt_ref, ybuf_ref, g_ref, lng_ref, lnb_ref, *rest, alpha, do_mod, tm):
    if do_mod:
        sc_ref, sh_ref, xo_ref, uo_ref, buf, sem = rest
    else:
        xo_ref, buf, sem = rest
    base = pl.program_id(0) * (tm * MOE_TOPK)

    def issue(t, carry):
        for k in range(MOE_TOPK):
            row = pos_ref[base + t * MOE_TOPK + k]
            pltpu.make_async_copy(ybuf_ref.at[row], buf.at[k, t], sem).start()
        return carry

    lax.fori_loop(0, tm, issue, 0, unroll=DMA_UNROLL)
    for k in range(MOE_TOPK):
        pltpu.make_async_copy(ybuf_ref.at[pl.ds(0, tm)], buf.at[k], sem).wait()
    gt = gt_ref[...]
    d = x_ref.shape[1]
    y = gt[:, 0:1] * buf[0].reshape(tm, d)
    for k in range(1, MOE_TOPK):
        y = y + gt[:, k:k + 1] * buf[k].reshape(tm, d)
    xn = _layer_norm(alpha * x_ref[...] + g_ref[...] * y, lng_ref[...], lnb_ref[...])
    xo_ref[...] = xn
    if do_mod:
        uo_ref[...] = xn * (1.0 + sc_ref[...]) + sh_ref[...]


def ln_combine(x, gates, pos, ybuf, ms, g_chunk, ln_g, ln_b, alpha, next_ms=None, sc_chunk=None, sh_chunk=None):
    n, d = x.shape
    tm = ms.tm
    do_mod = next_ms is not None
    row = pl.BlockSpec((tm, d), lambda i, p: (i, 0))
    vec = pl.BlockSpec((1, d), lambda i, p: (0, 0))
    in_specs = [row, pl.BlockSpec((tm, MOE_TOPK), lambda i, p: (i, 0)), pl.BlockSpec(memory_space=pl.ANY),
                ms.spec(g_chunk), vec, vec]
    args = [x, gates, ybuf, ms.array, ln_g.reshape(1, d), ln_b.reshape(1, d)]
    out_shape = [jax.ShapeDtypeStruct((n, d), F32)]
    out_specs = [row]
    if do_mod:
        in_specs += [next_ms.spec(sc_chunk), next_ms.spec(sh_chunk)]
        args += [next_ms.array, next_ms.array]
        out_shape.append(jax.ShapeDtypeStruct((n, d), F32))
        out_specs.append(row)
    res = pl.pallas_call(
        functools.partial(_ln_combine_kernel, alpha=alpha, do_mod=do_mod, tm=tm),
        out_shape=tuple(out_shape),
        grid_spec=pltpu.PrefetchScalarGridSpec(
            num_scalar_prefetch=1, grid=(n // tm,), in_specs=in_specs, out_specs=tuple(out_specs),
            scratch_shapes=[pltpu.VMEM((MOE_TOPK, tm, d // LANES, LANES), F32), pltpu.SemaphoreType.DMA]),
        compiler_params=_params("arbitrary"),
    )(pos, *args)
    return res if do_mod else (res[0], None)


def _dispatch_kernel(pos_ref, pad_ref, u3_ref, *rest, tm, n_e, tile, n_blk, first):
    xs_ref, zeros, sem, zsem = rest if first else rest[1:]
    i = pl.program_id(0)
    base = i * (tm * MOE_TOPK)

    def issue(t, carry):
        for k in range(MOE_TOPK):
            pltpu.make_async_copy(u3_ref.at[t], xs_ref.at[pos_ref[base + t * MOE_TOPK + k]], sem).start()
        return carry

    lax.fori_loop(0, tm, issue, 0, unroll=DMA_UNROLL)

    def zero_fill():
        zeros[...] = jnp.zeros_like(zeros)

        def fill_row(r, carry):
            pltpu.make_async_copy(zeros.at[0], xs_ref.at[r], zsem).start()
            return carry

        def drain_row(r, carry):
            pltpu.make_async_copy(zeros.at[0], xs_ref.at[0], zsem).wait()
            return carry

        def fill_blk(b, carry):
            pltpu.make_async_copy(zeros, xs_ref.at[pl.ds(b * tile, tile)], zsem).start()
            return carry

        def drain_blk(b, carry):
            pltpu.make_async_copy(zeros, xs_ref.at[pl.ds(0, tile)], zsem).wait()
            return carry

        for e in range(n_e):
            lax.fori_loop(pad_ref[e], pad_ref[n_e + e], fill_row, 0)
        lax.fori_loop(pad_ref[2 * n_e], n_blk, fill_blk, 0)
        for e in range(n_e):
            lax.fori_loop(pad_ref[e], pad_ref[n_e + e], drain_row, 0)
        lax.fori_loop(pad_ref[2 * n_e], n_blk, drain_blk, 0)

    if first:
        pl.when(i == 0)(zero_fill)
    for k in range(MOE_TOPK):
        pltpu.make_async_copy(u3_ref, xs_ref.at[pl.ds(0, tm)], sem).wait()


def dispatch(u3, pos, pad_bounds, xs, rows, tm, n_e, tile):
    n = u3.shape[0]
    tile_shape = u3.shape[1:]
    first = xs is None
    in_specs = [pl.BlockSpec((tm,) + tile_shape, lambda i, p, q: (i, 0, 0))]
    args = [pos, pad_bounds, u3]
    if not first:
        in_specs.append(pl.BlockSpec(memory_space=pl.ANY))
        args.append(xs)
    return pl.pallas_call(
        functools.partial(_dispatch_kernel, tm=tm, n_e=n_e, tile=tile, n_blk=rows // tile, first=first),
        out_shape=jax.ShapeDtypeStruct((rows,) + tile_shape, F32),
        grid_spec=pltpu.PrefetchScalarGridSpec(
            num_scalar_prefetch=2, grid=(n // tm,),
            in_specs=in_specs,
            out_specs=pl.BlockSpec(memory_space=pl.ANY),
            scratch_shapes=[pltpu.VMEM((tile,) + tile_shape, F32), pltpu.SemaphoreType.DMA,
                            pltpu.SemaphoreType.DMA]),
        input_output_aliases={} if first else {3: 0},
        compiler_params=_params("arbitrary"),
    )(*args)


def _experts_kernel(meta_ref, x_ref, w1_ref, b1_ref, w2_ref, b2_ref, o_ref, w1bf, w2bf, *, d_ff, tile, n_blk):
    i = pl.program_id(0)
    e = meta_ref[i]
    prev = meta_ref[jnp.maximum(i - 1, 0)]
    active = i * tile < meta_ref[n_blk]

    @pl.when(active & ((i == 0) | (e != prev)))
    def _():
        w1bf[...] = w1_ref[...].astype(BF16)
        w2bf[...] = w2_ref[...].astype(BF16)

    @pl.when(active)
    def _():
        x = x_ref[...].reshape(tile, w1bf.shape[0])
        h = jnp.dot(x.astype(BF16), w1bf[...], preferred_element_type=F32) + b1_ref[...]
        gt = jnp.minimum(h[:, :d_ff], SWIGLU_LIMIT)
        up = jnp.clip(h[:, d_ff:], -SWIGLU_LIMIT, SWIGLU_LIMIT)
        act = (up + 1.0) * (gt * jax.nn.sigmoid(SWIGLU_ALPHA * gt))
        y = jnp.dot(act.astype(BF16), w2bf[...], preferred_element_type=F32) + b2_ref[...]
        o_ref[...] = y.reshape(o_ref.shape)

    @pl.when(jnp.logical_not(active))
    def _():
        o_ref[...] = jnp.zeros_like(o_ref)


def experts(xs, meta, w1, b1, w2, b2, tile):
    rows = xs.shape[0]
    tile_shape = xs.shape[1:]
    n_e, d, ff2 = w1.shape
    d_ff = ff2 // 2
    n_blk = rows // tile
    xblk = pl.BlockSpec((tile,) + tile_shape, lambda i, m: (jnp.minimum(i, m[n_blk + 1]), 0, 0))
    return pl.pallas_call(
        functools.partial(_experts_kernel, d_ff=d_ff, tile=tile, n_blk=n_blk),
        out_shape=jax.ShapeDtypeStruct(xs.shape, F32),
        grid_spec=pltpu.PrefetchScalarGridSpec(
            num_scalar_prefetch=1, grid=(n_blk,),
            in_specs=[xblk,
                      pl.BlockSpec((None, d, ff2), lambda i, m: (m[i], 0, 0)),
                      pl.BlockSpec((None, 1, ff2), lambda i, m: (m[i], 0, 0)),
                      pl.BlockSpec((None, d_ff, d), lambda i, m: (m[i], 0, 0)),
                      pl.BlockSpec((None, 1, d), lambda i, m: (m[i], 0, 0))],
            out_specs=pl.BlockSpec((tile,) + tile_shape, lambda i, m: (i, 0, 0)),
            scratch_shapes=[pltpu.VMEM((d, ff2), BF16), pltpu.VMEM((d_ff, d), BF16)]),
        compiler_params=_params("arbitrary"),
    )(meta, xs, w1, b1.reshape(n_e, 1, ff2), w2, b2.reshape(n_e, 1, d))


def route_tables(counts, counts_first, tile, n_blk):
    n_e = counts.shape[0]
    padded = (counts + tile - 1) // tile * tile
    p_end = jnp.cumsum(padded)
    start = p_end - padded
    blk_e = jnp.minimum(jnp.searchsorted(p_end, jnp.arange(n_blk, dtype=jnp.int32) * tile, side='right'), n_e - 1)
    meta = jnp.concatenate([blk_e, p_end[-1:], p_end[-1:] // tile - 1]).astype(jnp.int32)
    pad_bounds = jnp.concatenate([start + counts_first, p_end, p_end[-1:] // tile]).astype(jnp.int32)
    return start.astype(jnp.int32), meta, pad_bounds


def _block_rank_select(gate, valid, block_id, n_blocks, axis):
    gate = jnp.where(valid, gate, NEG_INF)
    cnt = jnp.zeros(gate.shape, F32)
    for n2 in range(n_blocks):
        g2 = gate[n2:n2 + 1, :] if axis == 0 else gate[:, n2:n2 + 1]
        ahead = (g2 > gate) | ((g2 == gate) & (block_id > n2))
        cnt = cnt + jnp.where(ahead, 1.0, 0.0)
    return valid & (cnt < float(MOBA_TOPK))


MOBA_HEADS_PER_STEP = 2


def _moba_prompt_kernel(q_ref, k_ref, v_ref, o_ref, km_ref, kb_ref, vt_ref, sel_ref, *, n_blocks, blk, hd, scale):
    i = pl.program_id(2)
    heads = range(MOBA_HEADS_PER_STEP)
    cols = [slice(g * hd, (g + 1) * hd) for g in heads]

    @pl.when(i == 0)
    def _():
        for g in heads:
            for n in range(n_blocks):
                kblk = k_ref[n * blk:(n + 1) * blk, cols[g]]
                km_ref[g, n:n + 1, :] = jnp.mean(kblk, axis=0, keepdims=True)
                kb_ref[g, n * blk:(n + 1) * blk, :] = kblk.astype(BF16)
                vt_ref[g, n] = v_ref[n * blk:(n + 1) * blk, cols[g]].T.astype(BF16)

    start = pl.multiple_of(i * blk, blk)
    key_id = lax.broadcasted_iota(jnp.int32, (blk, blk), 0)
    q_id = lax.broadcasted_iota(jnp.int32, (blk, blk), 1)
    blk_id = lax.broadcasted_iota(jnp.int32, (n_blocks, blk), 0)
    qtb, state = [], []
    for g in heads:
        qt = q_ref[:, cols[g]].T
        gate = jnp.dot(km_ref[g], qt, precision=HIGHEST, preferred_element_type=F32)
        sel = _block_rank_select(gate, blk_id < i, blk_id, n_blocks, axis=0)
        sel_ref[g] = jnp.where(sel, 1.0, 0.0)
        qtb.append((qt * scale).astype(BF16))
        s = jnp.dot(kb_ref[g, pl.ds(start, blk), :], qtb[g], preferred_element_type=F32)
        s = jnp.where(key_id <= q_id, s, NEG_INF)
        m = jnp.max(s, axis=0, keepdims=True)
        p = jnp.exp(s - m)
        l = jnp.sum(p, axis=0, keepdims=True)
        acc = jnp.dot(vt_ref[g, i], p.astype(BF16), preferred_element_type=F32)
        state += [m, l, acc]

    def body(j, carry):
        n0 = 2 * j
        st = pl.multiple_of(n0 * blk, blk)
        out = []
        for g in heads:
            m, l, acc = carry[3 * g:3 * g + 3]
            s = jnp.dot(kb_ref[g, pl.ds(st, 2 * blk), :], qtb[g], preferred_element_type=F32)
            s0 = jnp.where(sel_ref[g, pl.ds(n0, 1), :] > 0.0, s[:blk], NEG_INF)
            s1 = jnp.where(sel_ref[g, pl.ds(n0 + 1, 1), :] > 0.0, s[blk:], NEG_INF)
            m_new = jnp.maximum(m, jnp.maximum(jnp.max(s0, axis=0, keepdims=True),
                                               jnp.max(s1, axis=0, keepdims=True)))
            a = jnp.exp(m - m_new)
            p0 = jnp.exp(s0 - m_new)
            p1 = jnp.exp(s1 - m_new)
            l = a * l + jnp.sum(p0, axis=0, keepdims=True) + jnp.sum(p1, axis=0, keepdims=True)
            acc = (a * acc + jnp.dot(vt_ref[g, n0], p0.astype(BF16), preferred_element_type=F32)
                   + jnp.dot(vt_ref[g, n0 + 1], p1.astype(BF16), preferred_element_type=F32))
            out += [m_new, l, acc]
        return tuple(out)

    state = lax.fori_loop(0, (i + 1) // 2, body, tuple(state))
    for g in heads:
        m, l, acc = state[3 * g:3 * g + 3]
        o_ref[:, cols[g]] = (acc / l).T


def moba_prompt(qkv, n_seq, seq, heads, hd):
    blk = MOBA_BLOCK
    hpg = MOBA_HEADS_PER_STEP
    assert seq % blk == 0 and heads % hpg == 0
    nq = seq // blk
    hg = heads // hpg
    return pl.pallas_call(
        functools.partial(_moba_prompt_kernel, n_blocks=nq, blk=blk, hd=hd, scale=hd ** -0.5),
        out_shape=jax.ShapeDtypeStruct((n_seq * seq, heads * hd), F32),
        grid=(n_seq, hg, nq),
        in_specs=[pl.BlockSpec((blk, hpg * hd), lambda b, h, i: (b * nq + i, h)),
                  pl.BlockSpec((seq, hpg * hd), lambda b, h, i: (b, hg + h)),
                  pl.BlockSpec((seq, hpg * hd), lambda b, h, i: (b, 2 * hg + h))],
        out_specs=pl.BlockSpec((blk, hpg * hd), lambda b, h, i: (b * nq + i, h)),
        scratch_shapes=[pltpu.VMEM((hpg, nq, hd), F32), pltpu.VMEM((hpg, seq, hd), BF16),
                        pltpu.VMEM((hpg, nq, hd, blk), BF16), pltpu.VMEM((hpg, nq, blk), F32)],
        compiler_params=_params("arbitrary", "arbitrary", "arbitrary"),
    )(qkv, qkv, qkv)


PAGES_PER_STEP = 4


def _moba_sample_kernel(pt_ref, q_ref, kn_ref, vn_ref, *rest, n_pages, page, heads, n_past_blocks, scale):
    k_refs = rest[:PAGES_PER_STEP]
    v_refs = rest[PAGES_PER_STEP:2 * PAGES_PER_STEP]
    o_ref, sc_ref, gate_ref, acc_ref, l_ref = rest[2 * PAGES_PER_STEP:]
    ph = pl.program_id(1)
    s = pl.program_id(2)
    pages_per_block = MOBA_BLOCK // page
    q = q_ref[...]
    R, hd = q.shape
    cols = page * heads
    q16 = (q * scale).astype(BF16)
    lane = lax.broadcasted_iota(jnp.int32, (R, LANES), 1)

    def head_of(shape, dim):
        return lax.rem(lax.broadcasted_iota(jnp.int32, shape, dim), heads)

    @pl.when((ph == 0) & (s == 0))
    def _():
        gate_ref[...] = jnp.zeros_like(gate_ref)

    @pl.when(ph == 0)
    def _():
        for r in range(PAGES_PER_STEP):
            k3 = k_refs[r][...]
            sc_ref[s * PAGES_PER_STEP + r] = _nt_dot(q16, k3.reshape(cols, hd).astype(BF16))
            ksum = jnp.sum(k3, axis=0) * (1.0 / MOBA_BLOCK)
            g = jnp.sum(q * jnp.concatenate([ksum] * (R // heads), axis=0), axis=-1, keepdims=True)
            blk_id = (s * PAGES_PER_STEP + r) // pages_per_block
            gate_ref[...] += jnp.where(lane == blk_id, g, 0.0)

    @pl.when((ph == 1) & (s == 0))
    def _():
        sel = _block_rank_select(gate_ref[...], lane < n_past_blocks, lane, n_past_blocks, axis=1)
        same_head = head_of((R, cols), 1) == head_of((R, cols), 0)
        s_own = _nt_dot(q16, kn_ref[...].astype(BF16))
        r_i = lax.broadcasted_iota(jnp.int32, (R, R), 0)
        c_i = lax.broadcasted_iota(jnp.int32, (R, R), 1)
        r_h, c_h = head_of((R, R), 0), head_of((R, R), 1)
        s_own = jnp.where((r_h == c_h) & (c_i - c_h <= r_i - r_h), s_own, NEG_INF)
        mx = jnp.full((R, cols), NEG_INF, F32)
        for pg in range(n_pages):
            b_id = pg // pages_per_block
            mx = jnp.maximum(mx, jnp.where(sel[:, b_id:b_id + 1] & same_head, sc_ref[pg], NEG_INF))
        m = jnp.maximum(jnp.max(s_own, axis=-1, keepdims=True), jnp.max(mx, axis=-1, keepdims=True))
        p_own = jnp.exp(s_own - m)
        psum = jnp.zeros((R, cols), F32)
        for pg in range(n_pages):
            b_id = pg // pages_per_block
            p = jnp.where(sel[:, b_id:b_id + 1] & same_head, jnp.exp(sc_ref[pg] - m), 0.0)
            sc_ref[pg] = p
            psum = psum + p
        l = jnp.sum(p_own, axis=-1, keepdims=True) + jnp.sum(psum, axis=-1, keepdims=True)
        l_ref[...] = jnp.broadcast_to(l, l_ref.shape)
        acc_ref[...] = jnp.dot(p_own.astype(BF16), vn_ref[...].astype(BF16), preferred_element_type=F32)

    @pl.when(ph == 1)
    def _():
        acc = acc_ref[...]
        for r in range(PAGES_PER_STEP):
            p = sc_ref[s * PAGES_PER_STEP + r]
            acc = acc + jnp.dot(p.astype(BF16), v_refs[r][...].reshape(cols, hd).astype(BF16),
                                preferred_element_type=F32)
        acc_ref[...] = acc

    @pl.when((ph == 1) & (s == pl.num_programs(2) - 1))
    def _():
        o_ref[...] = acc_ref[...] / l_ref[:, 0:1]


def moba_sample(qkv, cache_k, cache_v, page_table, layer_j, n_seq, t_new, heads, hd):
    d = heads * hd
    n_pool, n_a, page = cache_k.shape[0], cache_k.shape[1], cache_k.shape[2]
    n_pages = page_table.shape[1]
    past = n_pages * page
    assert past % MOBA_BLOCK == 0 and MOBA_BLOCK % page == 0 and t_new <= MOBA_BLOCK
    assert n_pages % PAGES_PER_STEP == 0
    n_steps = n_pages // PAGES_PER_STEP
    n_past_blocks = past // MOBA_BLOCK
    assert MOBA_TOPK <= n_past_blocks <= 128
    R = heads * t_new
    qkv3 = qkv.reshape(n_seq, t_new, 3 * d)
    q, k_new, v_new = (qkv3[:, :, c * d:(c + 1) * d].reshape(n_seq, R, hd) for c in range(3))
    pt = page_table.reshape(-1).astype(jnp.int32)

    def k_map(r):
        return lambda b, ph, s, pt: (pt[b * n_pages + jnp.where(ph == 0, s, n_steps - 1) * PAGES_PER_STEP + r],
                                     layer_j, 0, 0, 0)

    def v_map(r):
        return lambda b, ph, s, pt: (pt[b * n_pages + jnp.where(ph == 0, 0, s) * PAGES_PER_STEP + r],
                                     layer_j, 0, 0, 0)

    seq_blk = pl.BlockSpec((None, R, hd), lambda b, ph, s, pt: (b, 0, 0))
    page_blk = lambda index_map: pl.BlockSpec((None, None, page, heads, hd), index_map)
    o = pl.pallas_call(
        functools.partial(_moba_sample_kernel, n_pages=n_pages, page=page, heads=heads,
                          n_past_blocks=n_past_blocks, scale=hd ** -0.5),
        out_shape=jax.ShapeDtypeStruct((n_seq, R, hd), F32),
        grid_spec=pltpu.PrefetchScalarGridSpec(
            num_scalar_prefetch=1, grid=(n_seq, 2, n_steps),
            in_specs=[seq_blk, seq_blk, seq_blk]
                     + [page_blk(k_map(r)) for r in range(PAGES_PER_STEP)]
                     + [page_blk(v_map(r)) for r in range(PAGES_PER_STEP)],
            out_specs=seq_blk,
            scratch_shapes=[pltpu.VMEM((n_pages, R, page * heads), F32), pltpu.VMEM((R, LANES), F32),
                            pltpu.VMEM((R, hd), F32), pltpu.VMEM((R, LANES), F32)]),
        compiler_params=_params("arbitrary", "arbitrary", "arbitrary"),
    )(pt, q, k_new, v_new, *([cache_k] * PAGES_PER_STEP), *([cache_v] * PAGES_PER_STEP))
    return o.reshape(n_seq * t_new, d)


def _softplus(x):
    return jnp.maximum(x, 0.0) + jnp.log1p(jnp.exp(-jnp.abs(x)))


def _lru_gates(xc, wra_ref, bra, wix_ref, bix, sp, n, bw):
    cs = slice(n * bw, (n + 1) * bw)
    xcb = xc[:, cs]
    xcb16 = xcb.astype(BF16)
    r = jax.nn.sigmoid(jnp.dot(xcb16, wra_ref[n].astype(BF16), preferred_element_type=F32) + bra[:, cs])
    ig = jax.nn.sigmoid(jnp.dot(xcb16, wix_ref[n].astype(BF16), preferred_element_type=F32) + bix[:, cs])
    log_a = -LRU_C * r * sp[:, cs]
    a = jnp.exp(log_a)
    b = jnp.sqrt(jnp.tanh(-log_a) * (1.0 + a * a)) * (ig * xcb)
    return a, b


def _lru_prompt_kernel(yb_ref, xr_ref, cw_ref, cb_ref, wra_ref, bra_ref, wix_ref, bix_ref, lam_ref,
                       z_ref, cs_ref, hl_ref, xbuf, hcar, *, tc, n_lru_blocks, bw, conv_w):
    i = pl.program_id(1)

    @pl.when(i == 0)
    def _():
        xbuf[0:8, :] = jnp.zeros((8, xbuf.shape[1]), F32)
        hcar[...] = jnp.zeros_like(hcar)

    xr = xr_ref[...]
    xbuf[8:8 + tc, :] = xr
    cw = cw_ref[...]
    xc = cb_ref[...] + xr * cw[conv_w - 1:conv_w, :]
    for s in range(1, conv_w):
        xc = xc + xbuf[8 - s:8 - s + tc, :] * cw[conv_w - 1 - s:conv_w - s, :]
    xbuf[0:8, :] = xbuf[tc:tc + 8, :]
    sp = _softplus(-lam_ref[...])
    bra, bix = bra_ref[...], bix_ref[...]
    row = lax.broadcasted_iota(jnp.int32, (tc, bw), 0)
    for n in range(n_lru_blocks):
        cs = slice(n * bw, (n + 1) * bw)
        a, b = _lru_gates(xc, wra_ref, bra, wix_ref, bix, sp, n, bw)
        sh = 1
        while sh < tc:
            keep = row >= sh
            a_prev = jnp.where(keep, pltpu.roll(a, sh, 0), 1.0)
            b_prev = jnp.where(keep, pltpu.roll(b, sh, 0), 0.0)
            b = a * b_prev + b
            a = a * a_prev
            sh *= 2
        h = a * hcar[:, cs] + b
        hcar[:, cs] = h[tc - 1:tc, :]
        z_ref[:, cs] = h * jax.nn.gelu(yb_ref[:, cs])
    cs_ref[...] = xbuf[8 - (conv_w - 1):8, :]
    hl_ref[...] = hcar[...]


def lru_prompt(br, n_seq, seq, conv_w, conv_b, w_ra, b_ra, w_ix, b_ix, lam, tc=256):
    w = br.shape[1] // 2
    cw = conv_w.shape[0]
    nb, bw = w_ra.shape[0], w_ra.shape[1]
    assert seq % tc == 0 and cw - 1 <= 8 <= tc
    nt = seq // tc
    vec = pl.BlockSpec((1, w), lambda b, i: (0, 0))
    wblk = pl.BlockSpec((nb, bw, bw), lambda b, i: (0, 0, 0))
    return pl.pallas_call(
        functools.partial(_lru_prompt_kernel, tc=tc, n_lru_blocks=nb, bw=bw, conv_w=cw),
        out_shape=(jax.ShapeDtypeStruct((n_seq * seq, w), F32),
                   jax.ShapeDtypeStruct((n_seq, cw - 1, w), F32),
                   jax.ShapeDtypeStruct((n_seq, 1, w), F32)),
        grid=(n_seq, nt),
        in_specs=[pl.BlockSpec((tc, w), lambda b, i: (b * nt + i, 0)),
                  pl.BlockSpec((tc, w), lambda b, i: (b * nt + i, 1)),
                  pl.BlockSpec((cw, w), lambda b, i: (0, 0)), vec, wblk, vec, wblk, vec, vec],
        out_specs=(pl.BlockSpec((tc, w), lambda b, i: (b * nt + i, 0)),
                   pl.BlockSpec((None, cw - 1, w), lambda b, i: (b, 0, 0)),
                   pl.BlockSpec((None, 1, w), lambda b, i: (b, 0, 0))),
        scratch_shapes=[pltpu.VMEM((8 + tc, w), F32), pltpu.VMEM((1, w), F32)],
        compiler_params=_params("arbitrary", "arbitrary"),
    )(br, br, conv_w, conv_b.reshape(1, w), w_ra, b_ra.reshape(1, w), w_ix, b_ix.reshape(1, w), lam.reshape(1, w))


def _lru_sample_kernel(br_ref, c0_ref, h0_ref, cw_ref, cb_ref, wra_ref, bra_ref, wix_ref, bix_ref, lam_ref,
                       z_ref, cs_ref, hl_ref, *, t_new, n_lru_blocks, bw, conv_w, w):
    br = br_ref[...]
    yb, xr = br[:, :w], br[:, w:]
    xpad = jnp.concatenate([c0_ref[...], xr], axis=0)
    cw = cw_ref[...]
    xc = cb_ref[...] + xpad[0:t_new, :] * cw[0:1, :]
    for i in range(1, conv_w):
        xc = xc + xpad[i:i + t_new, :] * cw[i:i + 1, :]
    sp = _softplus(-lam_ref[...])
    bra, bix = bra_ref[...], bix_ref[...]
    gelu_y = jax.nn.gelu(yb)
    for n in range(n_lru_blocks):
        cs = slice(n * bw, (n + 1) * bw)
        a, b = _lru_gates(xc, wra_ref, bra, wix_ref, bix, sp, n, bw)
        h = h0_ref[:, cs]
        for t in range(t_new):
            h = a[t:t + 1, :] * h + b[t:t + 1, :]
            z_ref[t:t + 1, cs] = h * gelu_y[t:t + 1, cs]
        hl_ref[:, cs] = h
    cs_ref[...] = xpad[t_new:t_new + conv_w - 1, :]


def lru_sample(br, conv0, h0, n_seq, t_new, conv_w, conv_b, w_ra, b_ra, w_ix, b_ix, lam):
    w = br.shape[1] // 2
    cw = conv_w.shape[0]
    nb, bw = w_ra.shape[0], w_ra.shape[1]
    vec = pl.BlockSpec((1, w), lambda b: (0, 0))
    wblk = pl.BlockSpec((nb, bw, bw), lambda b: (0, 0, 0))
    per_seq = lambda rows, cols: pl.BlockSpec((None, rows, cols), lambda b: (b, 0, 0))
    z, cs, hl = pl.pallas_call(
        functools.partial(_lru_sample_kernel, t_new=t_new, n_lru_blocks=nb, bw=bw, conv_w=cw, w=w),
        out_shape=(jax.ShapeDtypeStruct((n_seq, t_new, w), F32),
                   jax.ShapeDtypeStruct((n_seq, cw - 1, w), F32),
                   jax.ShapeDtypeStruct((n_seq, 1, w), F32)),
        grid=(n_seq,),
        in_specs=[per_seq(t_new, 2 * w), per_seq(cw - 1, w), per_seq(1, w),
                  pl.BlockSpec((cw, w), lambda b: (0, 0)), vec, wblk, vec, wblk, vec, vec],
        out_specs=(per_seq(t_new, w), per_seq(cw - 1, w), per_seq(1, w)),
        compiler_params=_params("arbitrary"),
    )(br.reshape(n_seq, t_new, 2 * w), conv0, h0.reshape(n_seq, 1, w), conv_w, conv_b.reshape(1, w),
      w_ra, b_ra.reshape(1, w), w_ix, b_ix.reshape(1, w), lam.reshape(1, w))
    return z.reshape(n_seq * t_new, w), cs, hl


def _rope(x, cos, sin, half):
    x1, x2 = x[:, :half], x[:, half:]
    return jnp.concatenate([x1 * cos - x2 * sin, x2 * cos + x1 * sin], axis=-1)


def _group_norm_gate(o, g, gn_g, gn_b):
    mu = jnp.mean(o, axis=-1, keepdims=True)
    d = o - mu
    var = jnp.mean(d * d, axis=-1, keepdims=True)
    on = d * lax.rsqrt(var + LN_EPS) * gn_g + gn_b
    return (g * jax.nn.sigmoid(g)) * on


def _ret_prompt_kernel(q_ref, k_ref, v_ref, g_ref, cos_ref, sin_ref, dm_ref, cd_ref, kd_ref, chd_ref,
                       gng_ref, gnb_ref, z_ref, s_ref, S, *, kscale, half):
    c = pl.program_id(2)

    @pl.when(c == 0)
    def _():
        S[...] = jnp.zeros_like(S)

    cos, sin = cos_ref[...], sin_ref[...]
    q = _rope(q_ref[...], cos, sin, half)
    k = _rope(k_ref[...] * kscale, cos, sin, half)
    qb, kb, vb = q.astype(BF16), k.astype(BF16), v_ref[...].astype(BF16)
    inner = _nt_dot(qb, kb) * dm_ref[...]
    s_old = S[...]
    o = (jnp.dot(inner.astype(BF16), vb, preferred_element_type=F32)
         + jnp.dot(qb, s_old.astype(BF16), preferred_element_type=F32) * cd_ref[...])
    kdt = (k * kd_ref[...]).T.astype(BF16)
    S[...] = chd_ref[...] * s_old + jnp.dot(kdt, vb, preferred_element_type=F32)
    z_ref[...] = _group_norm_gate(o, g_ref[...], gng_ref[...], gnb_ref[...])

    @pl.when(c == pl.num_programs(2) - 1)
    def _():
        s_ref[...] = S[...]


def _decay_tables(heads, c):
    lg = jnp.log1p(-jnp.exp2(-5.0 - jnp.arange(heads, dtype=F32)))
    idx = jnp.arange(c, dtype=F32)
    diff = idx[:, None] - idx[None, :]
    dmask = jnp.where(diff >= 0, jnp.exp(jnp.maximum(diff, 0.0)[None] * lg[:, None, None]), 0.0)
    cross = jnp.exp((idx + 1.0)[None, :] * lg[:, None])[..., None]
    kdec = jnp.exp((c - 1.0 - idx)[None, :] * lg[:, None])[..., None]
    chunk = jnp.exp(c * lg)[:, None, None]
    return dmask, cross, kdec, chunk


def _rope_tables(pos, half):
    inv = ROPE_BASE ** (-jnp.arange(half, dtype=F32) / half)
    ang = pos.astype(F32)[:, None] * inv[None, :]
    return jnp.cos(ang), jnp.sin(ang)


def ret_prompt(proj, n_seq, seq, heads, kdim, vdim, gn_g, gn_b, tc=256):
    assert seq % tc == 0 and vdim % kdim == 0
    nc = seq // tc
    hk, hv = heads * kdim, heads * vdim
    half = kdim // 2
    cos, sin = _rope_tables(jnp.arange(seq), half)
    dmask, cross, kdec, chunk = _decay_tables(heads, tc)
    v0, g0 = 2 * hk // vdim, (2 * hk + hv) // vdim
    row = lambda b, h, c: b * nc + c
    per_head = lambda r, cdim: pl.BlockSpec((None, r, cdim), lambda b, h, c: (h, 0, 0))
    return pl.pallas_call(
        functools.partial(_ret_prompt_kernel, kscale=kdim ** -0.5, half=half),
        out_shape=(jax.ShapeDtypeStruct((n_seq * seq, hv), F32),
                   jax.ShapeDtypeStruct((n_seq, heads, kdim, vdim), F32)),
        grid=(n_seq, heads, nc),
        in_specs=[pl.BlockSpec((tc, kdim), lambda b, h, c: (row(b, h, c), h)),
                  pl.BlockSpec((tc, kdim), lambda b, h, c: (row(b, h, c), heads + h)),
                  pl.BlockSpec((tc, vdim), lambda b, h, c: (row(b, h, c), v0 + h)),
                  pl.BlockSpec((tc, vdim), lambda b, h, c: (row(b, h, c), g0 + h)),
                  pl.BlockSpec((tc, half), lambda b, h, c: (c, 0)),
                  pl.BlockSpec((tc, half), lambda b, h, c: (c, 0)),
                  per_head(tc, tc), per_head(tc, 1), per_head(tc, 1), per_head(1, 1),
                  pl.BlockSpec((1, vdim), lambda b, h, c: (0, h)),
                  pl.BlockSpec((1, vdim), lambda b, h, c: (0, h))],
        out_specs=(pl.BlockSpec((tc, vdim), lambda b, h, c: (row(b, h, c), h)),
                   pl.BlockSpec((None, None, kdim, vdim), lambda b, h, c: (b, h, 0, 0))),
        scratch_shapes=[pltpu.VMEM((kdim, vdim), F32)],
        compiler_params=_params("arbitrary", "arbitrary", "arbitrary"),
    )(proj, proj, proj, proj, cos, sin, dmask, cross, kdec, chunk, gn_g.reshape(1, hv), gn_b.reshape(1, hv))


def _ret_sample_kernel(p_ref, kt_ref, s0_ref, cos_ref, sin_ref, cost_ref, sint_ref, dm_ref, cd_ref, kd_ref, chd_ref,
                       gng_ref, gnb_ref, z_ref, s_ref, *, heads, kdim, vdim, t_new, kscale):
    half = kdim // 2
    hk, hv = heads * kdim, heads * vdim
    p = p_ref[...]
    cos, sin = cos_ref[...], sin_ref[...]
    cost, sint = cost_ref[...], sint_ref[...]
    for h in range(heads):
        q = _rope(p[:, h * kdim:(h + 1) * kdim], cos, sin, half)
        k = _rope(p[:, hk + h * kdim:hk + (h + 1) * kdim] * kscale, cos, sin, half)
        v = p[:, 2 * hk + h * vdim:2 * hk + (h + 1) * vdim]
        g = p[:, 2 * hk + hv + h * vdim:2 * hk + hv + (h + 1) * vdim]
        kt = kt_ref[h] * kscale
        k1, k2 = kt[:half, :], kt[half:, :]
        kt = jnp.concatenate([k1 * cost - k2 * sint, k2 * cost + k1 * sint], axis=0) * kd_ref[h]
        s0 = s0_ref[h]
        dm = dm_ref[h]
        o = jnp.dot(q, s0, preferred_element_type=F32, precision=HIGHEST) * cd_ref[h]
        s_new = chd_ref[h] * s0
        for m in range(t_new):
            inner_m = jnp.sum(q * k[m:m + 1, :], axis=-1, keepdims=True) * dm[:, m:m + 1]
            o = o + inner_m * v[m:m + 1, :]
            s_new = s_new + kt[:, m:m + 1] * v[m:m + 1, :]
        s_ref[h] = s_new
        z_ref[:, h * vdim:(h + 1) * vdim] = _group_norm_gate(
            o, g, gng_ref[:, h * vdim:(h + 1) * vdim], gnb_ref[:, h * vdim:(h + 1) * vdim])


def ret_sample(proj, s0, pos0, n_seq, t_new, heads, kdim, vdim, gn_g, gn_b):
    hk, hv = heads * kdim, heads * vdim
    half = kdim // 2
    width = 2 * hk + 2 * hv
    cos, sin = _rope_tables(pos0 + jnp.arange(t_new), half)
    dmask, cross, kdec, chunk = _decay_tables(heads, t_new)
    proj3 = proj.reshape(n_seq, t_new, width)
    kt = proj3[:, :, hk:2 * hk].reshape(n_seq, t_new, heads, kdim).transpose(0, 2, 3, 1)
    full = lambda *shape: pl.BlockSpec(shape, lambda b: (0,) * len(shape))
    z, s = pl.pallas_call(
        functools.partial(_ret_sample_kernel, heads=heads, kdim=kdim, vdim=vdim, t_new=t_new, kscale=kdim ** -0.5),
        out_shape=(jax.ShapeDtypeStruct((n_seq, t_new, hv), F32),
                   jax.ShapeDtypeStruct((n_seq, heads, kdim, vdim), F32)),
        grid=(n_seq,),
        in_specs=[pl.BlockSpec((None, t_new, width), lambda b: (b, 0, 0)),
                  pl.BlockSpec((None, heads, kdim, t_new), lambda b: (b, 0, 0, 0)),
                  pl.BlockSpec((None, heads, kdim, vdim), lambda b: (b, 0, 0, 0)),
                  full(t_new, half), full(t_new, half), full(half, t_new), full(half, t_new),
                  full(heads, t_new, t_new), full(heads, t_new, 1), full(heads, 1, t_new), full(heads, 1, 1),
                  full(1, hv), full(1, hv)],
        out_specs=(pl.BlockSpec((None, t_new, hv), lambda b: (b, 0, 0)),
                   pl.BlockSpec((None, heads, kdim, vdim), lambda b: (b, 0, 0, 0))),
        compiler_params=_params("arbitrary"),
    )(proj3, kt, s0, cos, sin, cos.T, sin.T, dmask, cross, kdec.transpose(0, 2, 1), chunk,
      gn_g.reshape(1, hv), gn_b.reshape(1, hv))
    return z.reshape(n_seq * t_new, hv), s


def kernel(x_prompt, x_sample, cache_k, cache_v, state_lru_conv, state_lru_h, state_ret, page_table, c_prompt, c_sample, w_ada, b_ada, ln_g, ln_b, a_w_qkv, a_w_o, lru_w_in, lru_b_in, lru_conv_w, lru_conv_b, lru_w_ra, lru_b_ra, lru_w_ix, lru_b_ix, lru_lambda, lru_w_out, ret_w_in, ret_gn_g, ret_gn_b, ret_w_o, moe_w_router, moe_b_router, moe_w1, moe_b1, moe_w2, moe_b2):
    depth, d = w_ada.shape[0], w_ada.shape[1]
    n_mixers = 3
    alpha = (2 * depth) ** 0.25
    heads, hd = cache_k.shape[3], cache_k.shape[4]
    r_heads, r_kdim, r_vdim = state_ret.shape[2], state_ret.shape[3], state_ret.shape[4]
    n_e = moe_w_router.shape[2]
    past_len = page_table.shape[1] * cache_k.shape[2]

    bp, bs = c_prompt.shape[0], c_sample.shape[0]
    n_c = bp + bs
    c_rows = 16 * (-(-n_c // 16))
    c_all = jnp.pad(jnp.concatenate([c_prompt, c_sample], axis=0), ((0, c_rows - n_c), (0, 0)))
    mods = [matmul(c_all, w_ada[l], b_ada[l], pre_silu=True, tn=1024) for l in range(depth)]

    class Group:
        def __init__(self, x, seq_lo, sample):
            self.sample = sample
            self.n_seq, self.t, _ = x.shape
            self.n = self.n_seq * self.t
            self.tm = min(ROW_TILE, self.n)
            self.ms = [ModSource(mods[l][seq_lo:seq_lo + self.n_seq], self.t, self.tm, d) for l in range(depth)]
            self.x2 = x.reshape(self.n, d)
            self.u = modulate(self.x2, self.ms[0], 1, 0)
            self.ks, self.vs, self.convs, self.hs, self.ss = [], [], [], [], []

        def outputs(self):
            return (self.x2.reshape(self.n_seq, self.t, d), jnp.stack(self.ks, 1), jnp.stack(self.vs, 1),
                    jnp.stack(self.convs, 1), jnp.stack(self.hs, 1), jnp.stack(self.ss, 1))

    def mixer(g, layer):
        kind, j = layer % n_mixers, layer // n_mixers
        u, sample, n_seq, t = g.u, g.sample, g.n_seq, g.t
        if kind == 0:
            if sample:
                qkv = matmul(u, a_w_qkv[j])
                o = moba_sample(qkv, cache_k, cache_v, page_table, j, n_seq, t, heads, hd)
                k3, v3 = qkv[:, d:2 * d], qkv[:, 2 * d:]
            else:
                qkv, k3, v3 = qkv_project(u, a_w_qkv[j], heads, hd)
                o = moba_prompt(qkv, n_seq, t, heads, hd)
            g.ks.append(k3.reshape(n_seq, t, heads, hd))
            g.vs.append(v3.reshape(n_seq, t, heads, hd))
            return matmul(o, a_w_o[j])
        if kind == 1:
            br = matmul(u, lru_w_in[j], lru_b_in[j])
            lru_args = (lru_conv_w[j], lru_conv_b[j], lru_w_ra[j], lru_b_ra[j], lru_w_ix[j], lru_b_ix[j],
                        lru_lambda[j])
            if sample:
                z, cs, hl = lru_sample(br, state_lru_conv[:, j], state_lru_h[:, j], n_seq, t, *lru_args)
            else:
                z, cs, hl = lru_prompt(br, n_seq, t, *lru_args)
            g.convs.append(cs)
            g.hs.append(hl.reshape(n_seq, -1))
            return matmul(z, lru_w_out[j])
        proj = matmul(u, ret_w_in[j])
        if sample:
            z, s_fin = ret_sample(proj, state_ret[:, j], past_len, n_seq, t, r_heads, r_kdim, r_vdim,
                                  ret_gn_g[j], ret_gn_b[j])
        else:
            z, s_fin = ret_prompt(proj, n_seq, t, r_heads, r_kdim, r_vdim, ret_gn_g[j], ret_gn_b[j])
        g.ss.append(s_fin)
        return matmul(z, ret_w_o[j])

    groups = [Group(x_prompt, 0, False), Group(x_sample, bp, True)]
    tile = MOE_TILE
    n_assign = sum(g.n for g in groups) * MOE_TOPK
    n_blk = -(-n_assign // tile) + n_e
    for layer in range(depth):
        counts = [jnp.zeros((1, n_e), jnp.int32)]
        routed = []
        for g in groups:
            y = mixer(g, layer)
            g.x2, u3, top_e, gates, rank, cnt = ln_router(
                g.x2, y, counts[-1], g.ms[layer], 2, 4, 3, ln_g[layer, 0], ln_b[layer, 0],
                moe_w_router[layer], moe_b_router[layer], alpha)
            counts.append(cnt)
            routed.append((u3, top_e, gates, rank))
        start, meta, pad_bounds = route_tables(counts[-1].reshape(n_e), counts[1].reshape(n_e), tile, n_blk)
        xs = None
        for g, (u3, top_e, gates, rank) in zip(groups, routed):
            g.pos = (jnp.take(start, top_e) + rank).reshape(g.n * MOE_TOPK)
            xs = dispatch(u3, g.pos, pad_bounds, xs, n_blk * tile, g.tm, n_e, tile)
        ybuf = experts(xs, meta, moe_w1[layer], moe_b1[layer], moe_w2[layer], moe_b2[layer], tile)
        for g, (u3, top_e, gates, rank) in zip(groups, routed):
            nxt = (g.ms[layer + 1], 1, 0) if layer + 1 < depth else ()
            g.x2, g.u = ln_combine(g.x2, gates, g.pos, ybuf, g.ms[layer], 5, ln_g[layer, 1], ln_b[layer, 1], alpha,
                                   *nxt)

    y_p, k_p, v_p, conv_p, h_p, s_p = groups[0].outputs()
    y_s, k_s, v_s, conv_s, h_s, s_s = groups[1].outputs()
    return (y_p, y_s, k_p, v_p, conv_p, h_p, s_p, k_s, v_s, conv_s, h_s, s_s)
```

```python
import functools

import numpy as np
import jax
import jax.numpy as jnp
from jax import lax
from jax.experimental import pallas as pl
from jax.experimental.pallas import tpu as pltpu

F32 = jnp.float32
BF16 = jnp.bfloat16
HIGHEST = lax.Precision.HIGHEST
NEG_INF = float("-inf")

MOBA_BLOCK = 256
MOBA_TOPK = 3
MOE_TOPK = 4
LRU_C = 8.0
ROPE_BASE = 10000.0
SWIGLU_LIMIT = 7.0
SWIGLU_ALPHA = 1.702
LN_EPS = 1e-5

LANES = 128
DMA_UNROLL = 8
MOE_TILE = 256
ROW_TILE = 256
V7X_VMEM_BYTES = 64 * 2**20
VMEM_LIMIT = V7X_VMEM_BYTES - 8 * 2**20


def _params(*sem):
    return pltpu.CompilerParams(dimension_semantics=sem, vmem_limit_bytes=VMEM_LIMIT)


def _nt_dot(a, b, **kw):
    return lax.dot_general(a, b, (((1,), (1,)), ((), ())), preferred_element_type=F32, **kw)


def _mm_kernel(*refs, has_bias, pre_silu):
    if has_bias:
        x_ref, w_ref, b_ref, o_ref, wbf_ref = refs
    else:
        x_ref, w_ref, o_ref, wbf_ref = refs

    @pl.when(pl.program_id(1) == 0)
    def _():
        wbf_ref[...] = w_ref[...].astype(BF16)

    x = x_ref[...]
    if pre_silu:
        x = x * jax.nn.sigmoid(x)
    acc = jnp.dot(x.astype(BF16), wbf_ref[...], preferred_element_type=F32)
    if has_bias:
        acc = acc + b_ref[...]
    o_ref[...] = acc


def matmul(x, w, b=None, *, pre_silu=False, tm=1024, tn=1024):
    M, K = x.shape
    N = w.shape[1]
    tm, tn = min(tm, M), min(tn, N)
    assert M % tm == 0 and N % tn == 0, (M, N, tm, tn)
    in_specs = [pl.BlockSpec((tm, K), lambda j, i: (i, 0)),
                pl.BlockSpec((K, tn), lambda j, i: (0, j))]
    args = [x, w]
    if b is not None:
        in_specs.append(pl.BlockSpec((1, tn), lambda j, i: (0, j)))
        args.append(b.reshape(1, N))
    return pl.pallas_call(
        functools.partial(_mm_kernel, has_bias=b is not None, pre_silu=pre_silu),
        out_shape=jax.ShapeDtypeStruct((M, N), F32),
        grid=(N // tn, M // tm),
        in_specs=in_specs,
        out_specs=pl.BlockSpec((tm, tn), lambda j, i: (i, j)),
        scratch_shapes=[pltpu.VMEM((K, tn), BF16)],
        compiler_params=_params("arbitrary", "arbitrary"),
    )(*args)


def _qkv_kernel(x_ref, w_ref, *rest, d, n_prev):
    prev_k, prev_v = rest[:n_prev], rest[n_prev:2 * n_prev]
    qkv_ref, k_ref, v_ref, wbf_ref = rest[2 * n_prev:]

    @pl.when(pl.program_id(0) == 0)
    def _():
        wbf_ref[...] = w_ref[...].astype(BF16)

    acc = jnp.dot(x_ref[...].astype(BF16), wbf_ref[...], preferred_element_type=F32)
    qkv_ref[...] = acc
    leaf = k_ref.shape[-3:]
    k_new, v_new = acc[:, d:2 * d].reshape(leaf), acc[:, 2 * d:].reshape(leaf)
    if n_prev == 0:
        k_ref[...] = k_new
        v_ref[...] = v_new
    else:
        for p in range(n_prev):
            k_ref[p] = prev_k[p][...]
            v_ref[p] = prev_v[p][...]
        k_ref[n_prev] = k_new
        v_ref[n_prev] = v_new


def qkv_project(u, w, heads, hd, n_seq, seq, prev_k=(), prev_v=(), tm=256):
    n, d = u.shape
    assert n == n_seq * seq and seq % tm == 0 and w.shape == (d, 3 * d) and heads * hd == d
    n_prev = len(prev_k)
    tps = seq // tm
    tok_blk = pl.BlockSpec((tm, heads, hd), lambda i: (i, 0, 0))
    if n_prev == 0:
        leaf, leaf_blk = jax.ShapeDtypeStruct((n, heads, hd), F32), tok_blk
    else:
        leaf = jax.ShapeDtypeStruct((n_seq, n_prev + 1, seq, heads, hd), F32)
        leaf_blk = pl.BlockSpec((None, n_prev + 1, tm, heads, hd), lambda i: (i // tps, 0, i % tps, 0, 0))
    return pl.pallas_call(
        functools.partial(_qkv_kernel, d=d, n_prev=n_prev),
        out_shape=(jax.ShapeDtypeStruct((n, 3 * d), F32), leaf, leaf),
        grid=(n // tm,),
        in_specs=[pl.BlockSpec((tm, d), lambda i: (i, 0)),
                  pl.BlockSpec((d, 3 * d), lambda i: (0, 0), pipeline_mode=pl.Buffered(1))]
                 + [tok_blk] * (2 * n_prev),
        out_specs=(pl.BlockSpec((tm, 3 * d), lambda i: (i, 0)), leaf_blk, leaf_blk),
        scratch_shapes=[pltpu.VMEM((d, 3 * d), BF16)],
        compiler_params=_params("arbitrary"),
    )(u, w, *prev_k, *prev_v)


class ModSource:
    def __init__(self, mod, rows_per_seq, tm, d):
        self.d = d
        self.tm = tm
        if rows_per_seq % tm == 0:
            self.per_seq = True
            self.tiles_per_seq = rows_per_seq // tm
            self.array = mod.reshape(mod.shape[0], 1, mod.shape[1])
        else:
            self.per_seq = False
            self.array = jnp.repeat(mod, rows_per_seq, axis=0)

    def spec(self, chunk):
        if self.per_seq:
            tps = self.tiles_per_seq
            return pl.BlockSpec((None, 1, self.d), lambda i, *_: (i // tps, 0, chunk))
        return pl.BlockSpec((self.tm, self.d), lambda i, *_: (i, chunk))


def _layer_norm(z, g, b):
    mu = jnp.mean(z, axis=-1, keepdims=True)
    d = z - mu
    var = jnp.mean(d * d, axis=-1, keepdims=True)
    return d * lax.rsqrt(var + LN_EPS) * g + b


def _router(u, wr, br, e_ref, gt_ref, rk_ref, cnt_ref, carry):
    logits = jnp.dot(u, wr, preferred_element_type=F32, precision=HIGHEST) + br
    tm, n_e = logits.shape
    lane = lax.broadcasted_iota(jnp.int32, logits.shape, 1).astype(F32)
    slot = lax.broadcasted_iota(jnp.int32, (tm, MOE_TOPK), 1)
    idx_out = jnp.zeros((tm, MOE_TOPK), F32)
    val_out = jnp.zeros((tm, MOE_TOPK), F32)
    cur = logits
    top = None
    picks = []
    for k in range(MOE_TOPK):
        m = jnp.max(cur, axis=-1, keepdims=True)
        idx = jnp.min(jnp.where(cur == m, lane, float(n_e)), axis=-1, keepdims=True)
        if top is None:
            top = m
        idx_out = jnp.where(slot == k, idx, idx_out)
        val_out = jnp.where(slot == k, m, val_out)
        picks.append(lane == idx)
        cur = jnp.where(picks[-1], NEG_INF, cur)
    ex = jnp.exp(val_out - top)
    gt_ref[...] = ex / jnp.sum(ex, axis=-1, keepdims=True)
    e_ref[...] = idx_out.astype(jnp.int32)
    hot = sum(jnp.where(pk, 1.0, 0.0) for pk in picks)
    r_id = lax.broadcasted_iota(jnp.int32, (tm, tm), 0)
    c_id = lax.broadcasted_iota(jnp.int32, (tm, tm), 1)
    tri = jnp.where(c_id < r_id, 1.0, 0.0).astype(BF16)
    before = jnp.dot(tri, hot.astype(BF16), preferred_element_type=F32) + carry[...]
    rk_out = jnp.zeros((tm, MOE_TOPK), F32)
    for k in range(MOE_TOPK):
        rk = jnp.sum(jnp.where(picks[k], before, 0.0), axis=-1, keepdims=True)
        rk_out = jnp.where(slot == k, rk, rk_out)
    rk_ref[...] = rk_out.astype(jnp.int32)
    carry[...] += jnp.sum(hot, axis=0, keepdims=True)
    cnt_ref[...] = carry[...].astype(jnp.int32)


def _mod_kernel(x_ref, sc_ref, sh_ref, u_ref):
    u_ref[...] = (x_ref[...] * (1.0 + sc_ref[...]) + sh_ref[...]).astype(u_ref.dtype)


def modulate(x, ms, sc_chunk, sh_chunk):
    n, d = x.shape
    tm = ms.tm
    row = pl.BlockSpec((tm, d), lambda i: (i, 0))
    return pl.pallas_call(
        _mod_kernel,
        out_shape=jax.ShapeDtypeStruct((n, d), BF16),
        grid=(n // tm,),
        in_specs=[row, ms.spec(sc_chunk), ms.spec(sh_chunk)],
        out_specs=row,
        compiler_params=_params("arbitrary"),
    )(x, ms.array, ms.array)


def _ln_router_kernel(x_ref, y_ref, cnt0_ref, g_ref, lng_ref, lnb_ref, sc_ref, sh_ref, wr_ref, br_ref,
                      xo_ref, u3_ref, e_ref, gt_ref, rk_ref, cnt_ref, carry, *, alpha):
    @pl.when(pl.program_id(0) == 0)
    def _():
        carry[...] = cnt0_ref[...].astype(F32)

    xn = _layer_norm(alpha * x_ref[...] + g_ref[...] * y_ref[...], lng_ref[...], lnb_ref[...])
    xo_ref[...] = xn
    u = xn * (1.0 + sc_ref[...]) + sh_ref[...]
    u3_ref[...] = u.reshape(u3_ref.shape)
    _router(u, wr_ref[...], br_ref[...], e_ref, gt_ref, rk_ref, cnt_ref, carry)


def ln_router(x, y, counts0, ms, g_chunk, sc_chunk, sh_chunk, ln_g, ln_b, w_r, b_r, alpha):
    n, d = x.shape
    tm = ms.tm
    n_e = w_r.shape[1]
    row = pl.BlockSpec((tm, d), lambda i: (i, 0))
    vec = pl.BlockSpec((1, d), lambda i: (0, 0))
    kk = pl.BlockSpec((tm, MOE_TOPK), lambda i: (i, 0))
    kk_i = jax.ShapeDtypeStruct((n, MOE_TOPK), jnp.int32)
    return pl.pallas_call(
        functools.partial(_ln_router_kernel, alpha=alpha),
        out_shape=(jax.ShapeDtypeStruct((n, d), F32), jax.ShapeDtypeStruct((n, d // LANES, LANES), F32),
                   kk_i, jax.ShapeDtypeStruct((n, MOE_TOPK), F32), kk_i,
                   jax.ShapeDtypeStruct((1, n_e), jnp.int32)),
        grid=(n // tm,),
        in_specs=[row, row, pl.BlockSpec((1, n_e), lambda i: (0, 0)), ms.spec(g_chunk), vec, vec,
                  ms.spec(sc_chunk), ms.spec(sh_chunk),
                  pl.BlockSpec((d, n_e), lambda i: (0, 0)), pl.BlockSpec((1, n_e), lambda i: (0, 0))],
        out_specs=(row, pl.BlockSpec((tm, d // LANES, LANES), lambda i: (i, 0, 0)), kk, kk, kk,
                   pl.BlockSpec((1, n_e), lambda i: (0, 0))),
        scratch_shapes=[pltpu.VMEM((1, n_e), F32)],
        compiler_params=_params("arbitrary"),
    )(x, y, counts0, ms.array, ln_g.reshape(1, d), ln_b.reshape(1, d), ms.array, ms.array, w_r,
      b_r.reshape(1, n_e))


def _ln_combine_kernel(x_ref, gt_ref, ytok_ref, g_ref, lng_ref, lnb_ref, *rest, alpha, do_mod, tm):
    if do_mod:
        sc_ref, sh_ref, xo_ref, uo_ref = rest
    else:
        (xo_ref,) = rest
    gt = gt_ref[...]
    d = x_ref.shape[1]
    yt = ytok_ref[...].reshape((tm, MOE_TOPK) + ytok_ref.shape[1:])
    y = gt[:, 0:1] * yt[:, 0].reshape(tm, d)
    for k in range(1, MOE_TOPK):
        y = y + gt[:, k:k + 1] * yt[:, k].reshape(tm, d)
    xn = _layer_norm(alpha * x_ref[...] + g_ref[...] * y, lng_ref[...], lnb_ref[...])
    xo_ref[...] = xn
    if do_mod:
        uo_ref[...] = (xn * (1.0 + sc_ref[...]) + sh_ref[...]).astype(uo_ref.dtype)


def ln_combine(x, gates, ytok, first_assign, ms, g_chunk, ln_g, ln_b, alpha, next_ms=None, sc_chunk=None,
               sh_chunk=None):
    n, d = x.shape
    tm = ms.tm
    do_mod = next_ms is not None
    assert first_assign % (tm * MOE_TOPK) == 0
    blk0 = first_assign // (tm * MOE_TOPK)
    row = pl.BlockSpec((tm, d), lambda i: (i, 0))
    vec = pl.BlockSpec((1, d), lambda i: (0, 0))
    in_specs = [row, pl.BlockSpec((tm, MOE_TOPK), lambda i: (i, 0)),
                pl.BlockSpec((tm * MOE_TOPK,) + ytok.shape[1:], lambda i: (blk0 + i, 0, 0)),
                ms.spec(g_chunk), vec, vec]
    args = [x, gates, ytok, ms.array, ln_g.reshape(1, d), ln_b.reshape(1, d)]
    out_shape = [jax.ShapeDtypeStruct((n, d), F32)]
    out_specs = [row]
    if do_mod:
        in_specs += [next_ms.spec(sc_chunk), next_ms.spec(sh_chunk)]
        args += [next_ms.array, next_ms.array]
        out_shape.append(jax.ShapeDtypeStruct((n, d), BF16))
        out_specs.append(row)
    res = pl.pallas_call(
        functools.partial(_ln_combine_kernel, alpha=alpha, do_mod=do_mod, tm=tm),
        out_shape=tuple(out_shape),
        grid=(n // tm,), in_specs=in_specs, out_specs=tuple(out_specs),
        compiler_params=_params("arbitrary"),
    )(*args)
    return res if do_mod else (res[0], None)


def _dispatch_kernel(pos_ref, pad_ref, u3_ref, *rest, tm, n_e, tile, n_blk, n_assign, first):
    if first:
        xs_ref, inv_ref, zeros, sem, zsem = rest
    else:
        _, xs_ref, zeros, sem, zsem = rest
    i = pl.program_id(0)
    base = i * (tm * MOE_TOPK)

    def issue(t, carry):
        for k in range(MOE_TOPK):
            a = base + t * MOE_TOPK + k
            row = pos_ref[a]
            pltpu.make_async_copy(u3_ref.at[t], xs_ref.at[row], sem).start()
            if first:
                inv_ref[row] = a
        return carry

    lax.fori_loop(0, tm, issue, 0, unroll=DMA_UNROLL)

    def zero_fill():
        zeros[...] = jnp.zeros_like(zeros)

        def fill_row(r, carry):
            pltpu.make_async_copy(zeros.at[0], xs_ref.at[r], zsem).start()
            return carry

        def fill_later_row(r, carry):
            inv_ref[r] = 0
            return fill_row(r, carry)

        def fill_pad_row(r, spare):
            inv_ref[r] = spare
            return fill_row(r, spare + 1)

        def drain_row(r, carry):
            pltpu.make_async_copy(zeros.at[0], xs_ref.at[0], zsem).wait()
            return carry

        def fill_blk(b, spare):
            pltpu.make_async_copy(zeros, xs_ref.at[pl.ds(b * tile, tile)], zsem).start()

            def mark(r, s):
                inv_ref[b * tile + r] = s
                return s + 1

            return lax.fori_loop(0, tile, mark, spare)

        def drain_blk(b, carry):
            pltpu.make_async_copy(zeros, xs_ref.at[pl.ds(0, tile)], zsem).wait()
            return carry

        spare = n_assign
        for e in range(n_e):
            lax.fori_loop(pad_ref[e], pad_ref[n_e + e], fill_later_row, 0)
            spare = lax.fori_loop(pad_ref[n_e + e], pad_ref[2 * n_e + e], fill_pad_row, spare)
        lax.fori_loop(pad_ref[3 * n_e], n_blk, fill_blk, spare)
        for e in range(n_e):
            lax.fori_loop(pad_ref[e], pad_ref[2 * n_e + e], drain_row, 0)
        lax.fori_loop(pad_ref[3 * n_e], n_blk, drain_blk, 0)

    if first:
        pl.when(i == 0)(zero_fill)
    for k in range(MOE_TOPK):
        pltpu.make_async_copy(u3_ref, xs_ref.at[pl.ds(0, tm)], sem).wait()


def dispatch(u3, pos, pad_bounds, xs, rows, tm, n_e, tile, n_assign):
    n = u3.shape[0]
    tile_shape = u3.shape[1:]
    first = xs is None
    in_specs = [pl.BlockSpec((tm,) + tile_shape, lambda i, p, q: (i, 0, 0))]
    args = [pos, pad_bounds, u3]
    xs_shape = jax.ShapeDtypeStruct((rows,) + tile_shape, F32)
    any_spec = pl.BlockSpec(memory_space=pl.ANY)
    if first:
        out_shape = (xs_shape, jax.ShapeDtypeStruct((rows,), jnp.int32))
        out_specs = (any_spec, pl.BlockSpec(memory_space=pltpu.SMEM))
    else:
        in_specs.append(any_spec)
        args.append(xs)
        out_shape, out_specs = xs_shape, any_spec
    return pl.pallas_call(
        functools.partial(_dispatch_kernel, tm=tm, n_e=n_e, tile=tile, n_blk=rows // tile, n_assign=n_assign,
                          first=first),
        out_shape=out_shape,
        grid_spec=pltpu.PrefetchScalarGridSpec(
            num_scalar_prefetch=2, grid=(n // tm,),
            in_specs=in_specs,
            out_specs=out_specs,
            scratch_shapes=[pltpu.VMEM((tile,) + tile_shape, F32), pltpu.SemaphoreType.DMA,
                            pltpu.SemaphoreType.DMA]),
        input_output_aliases={} if first else {3: 0},
        compiler_params=_params("arbitrary"),
    )(*args)


def _experts_kernel(blk_e_ref, inv_ref, x_ref, w1_ref, b1_ref, w2_ref, b2_ref, ytok_ref, w1bf, w2bf, obuf, sem,
                    *, d_ff, tile, n_blk):
    i = pl.program_id(0)
    blk = jnp.minimum(i, n_blk - 1)
    e = blk_e_ref[blk]
    prev = blk_e_ref[jnp.maximum(blk - 1, 0)]

    @pl.when((i == 0) | (e != prev))
    def _():
        w1bf[...] = w1_ref[...].astype(BF16)
        w2bf[...] = w2_ref[...].astype(BF16)

    def compute():
        x = x_ref[...].reshape(tile, w1bf.shape[0])
        h = jnp.dot(x.astype(BF16), w1bf[...], preferred_element_type=F32) + b1_ref[...]
        gt = jnp.minimum(h[:, :d_ff], SWIGLU_LIMIT)
        up = jnp.clip(h[:, d_ff:], -SWIGLU_LIMIT, SWIGLU_LIMIT)
        act = (up + 1.0) * (gt * jax.nn.sigmoid(SWIGLU_ALPHA * gt))
        y = jnp.dot(act.astype(BF16), w2bf[...], preferred_element_type=F32) + b2_ref[...]
        obuf[i % 2] = y.reshape(obuf.shape[1:])

    def start_scatter():
        half = (i + 1) % 2
        for r in range(tile):
            pltpu.make_async_copy(obuf.at[half, r], ytok_ref.at[inv_ref[(i - 1) * tile + r]], sem).start()

    def wait_scatter():
        pltpu.make_async_copy(obuf.at[0], ytok_ref.at[pl.ds(0, tile)], sem).wait()

    @pl.when(i == 0)
    def _():
        compute()

    @pl.when((i > 0) & (i < n_blk))
    def _():
        start_scatter()
        compute()
        wait_scatter()

    @pl.when(i == n_blk)
    def _():
        start_scatter()
        wait_scatter()


def experts(xs, blk_e, inv, w1, b1, w2, b2, tile):
    rows = xs.shape[0]
    tile_shape = xs.shape[1:]
    n_e, d, ff2 = w1.shape
    d_ff = ff2 // 2
    n_blk = rows // tile
    cur = lambda i, m: jnp.minimum(i, n_blk - 1)
    return pl.pallas_call(
        functools.partial(_experts_kernel, d_ff=d_ff, tile=tile, n_blk=n_blk),
        out_shape=jax.ShapeDtypeStruct(xs.shape, F32),
        grid_spec=pltpu.PrefetchScalarGridSpec(
            num_scalar_prefetch=2, grid=(n_blk + 1,),
            in_specs=[pl.BlockSpec((tile,) + tile_shape, lambda i, m, v: (cur(i, m), 0, 0)),
                      pl.BlockSpec((None, d, ff2), lambda i, m, v: (m[cur(i, m)], 0, 0)),
                      pl.BlockSpec((None, 1, ff2), lambda i, m, v: (m[cur(i, m)], 0, 0)),
                      pl.BlockSpec((None, d_ff, d), lambda i, m, v: (m[cur(i, m)], 0, 0)),
                      pl.BlockSpec((None, 1, d), lambda i, m, v: (m[cur(i, m)], 0, 0))],
            out_specs=pl.BlockSpec(memory_space=pl.ANY),
            scratch_shapes=[pltpu.VMEM((d, ff2), BF16), pltpu.VMEM((d_ff, d), BF16),
                            pltpu.VMEM((2, tile) + tile_shape, F32), pltpu.SemaphoreType.DMA]),
        compiler_params=_params("arbitrary"),
    )(blk_e, inv, xs, w1, b1.reshape(n_e, 1, ff2), w2, b2.reshape(n_e, 1, d))


def route_tables(counts, counts_first, tile, n_blk):
    n_e = counts.shape[0]
    padded = (counts + tile - 1) // tile * tile
    p_end = jnp.cumsum(padded)
    start = p_end - padded
    blk_start = jnp.arange(n_blk, dtype=jnp.int32) * tile
    blk_e = jnp.minimum(jnp.sum((p_end[None, :] <= blk_start[:, None]).astype(jnp.int32), axis=1), n_e - 1)
    pad_bounds = jnp.concatenate([start + counts_first, start + counts, p_end, p_end[-1:] // tile])
    return start.astype(jnp.int32), blk_e.astype(jnp.int32), pad_bounds.astype(jnp.int32)


def _block_rank_select(gate, valid, block_id, n_blocks, axis):
    gate = jnp.where(valid, gate, NEG_INF)
    cnt = jnp.zeros(gate.shape, F32)
    for n2 in range(n_blocks):
        g2 = gate[n2:n2 + 1, :] if axis == 0 else gate[:, n2:n2 + 1]
        ahead = (g2 > gate) | ((g2 == gate) & (block_id > n2))
        cnt = cnt + jnp.where(ahead, 1.0, 0.0)
    return valid & (cnt < float(MOBA_TOPK))


MOBA_HEADS_PER_STEP = 2


def _moba_prompt_kernel(q_ref, k_ref, v_ref, o_ref, km_ref, kb_ref, vt_ref, sel_ref, acc_ref, s_ref,
                        *, n_blocks, blk, hd, scale):
    i = pl.program_id(2)
    heads = range(MOBA_HEADS_PER_STEP)
    cols = [slice(g * hd, (g + 1) * hd) for g in heads]

    @pl.when(i == 0)
    def _():
        for g in heads:
            for n in range(n_blocks):
                kblk = k_ref[n * blk:(n + 1) * blk, cols[g]]
                km_ref[g, n:n + 1, :] = jnp.mean(kblk, axis=0, keepdims=True)
                kb_ref[g, n * blk:(n + 1) * blk, :] = kblk.astype(BF16)
                vt_ref[g, n] = v_ref[n * blk:(n + 1) * blk, cols[g]].T.astype(BF16)

    start = pl.multiple_of(i * blk, blk)
    key_id = lax.broadcasted_iota(jnp.int32, (blk, LANES), 0)
    q_id = lax.broadcasted_iota(jnp.int32, (blk, LANES), 1)
    blk_id = lax.broadcasted_iota(jnp.int32, (n_blocks, blk), 0)
    chains = [(g, hf) for g in heads for hf in range(blk // LANES)]
    qtb, state = [], []
    for c, (g, hf) in enumerate(chains):
        qcols = slice(hf * LANES, (hf + 1) * LANES)
        if hf == 0:
            qt = q_ref[:, cols[g]].T
            gate = jnp.dot(km_ref[g], qt, precision=HIGHEST, preferred_element_type=F32)
            sel = jnp.where(_block_rank_select(gate, blk_id < i, blk_id, n_blocks, axis=0), 1.0, 0.0)
            qs = (qt * scale).astype(BF16)
        sel_ref[c] = sel[:, qcols]
        qtb.append(qs[:, qcols])
        s = jnp.dot(kb_ref[g, pl.ds(start, blk), :], qtb[c], preferred_element_type=F32)
        s = jnp.where(key_id <= q_id + hf * LANES, s, NEG_INF)
        m = jnp.max(s, axis=0, keepdims=True)
        p = jnp.exp(s - m)
        acc_ref[c] = jnp.dot(vt_ref[g, i], p.astype(BF16), preferred_element_type=F32)
        state += [m, jnp.sum(p, axis=0, keepdims=True)]
        s_ref[c] = jnp.dot(kb_ref[g, 0:blk, :], qtb[c], preferred_element_type=F32)

    def body(n, carry):
        nxt = pl.multiple_of((n + 1) * blk, blk)
        s_cur = [s_ref[c] for c in range(len(chains))]
        s_nxt = [jnp.dot(kb_ref[g, pl.ds(nxt, blk), :], qtb[c], preferred_element_type=F32)
                 for c, (g, hf) in enumerate(chains)]
        out = []
        for c, (g, hf) in enumerate(chains):
            m, l = carry[2 * c:2 * c + 2]
            s = jnp.where(sel_ref[c, pl.ds(n, 1), :] > 0.0, s_cur[c], NEG_INF)
            m_new = jnp.maximum(m, jnp.max(s, axis=0, keepdims=True))
            a = jnp.exp(m - m_new)
            p = jnp.exp(s - m_new)
            acc_ref[c] = a * acc_ref[c] + jnp.dot(vt_ref[g, n], p.astype(BF16), preferred_element_type=F32)
            out += [m_new, a * l + jnp.sum(p, axis=0, keepdims=True)]
        for c in range(len(chains)):
            s_ref[c] = s_nxt[c]
        return tuple(out)

    state = lax.fori_loop(0, i, body, tuple(state))
    for c, (g, hf) in enumerate(chains):
        o_ref[hf * LANES:(hf + 1) * LANES, cols[g]] = (acc_ref[c] / state[2 * c + 1]).T.astype(o_ref.dtype)


def moba_prompt(qkv, n_seq, seq, heads, hd):
    blk = MOBA_BLOCK
    hpg = MOBA_HEADS_PER_STEP
    assert seq % blk == 0 and heads % hpg == 0
    nq = seq // blk
    hg = heads // hpg
    return pl.pallas_call(
        functools.partial(_moba_prompt_kernel, n_blocks=nq, blk=blk, hd=hd, scale=hd ** -0.5),
        out_shape=jax.ShapeDtypeStruct((n_seq * seq, heads * hd), BF16),
        grid=(n_seq, hg, nq),
        in_specs=[pl.BlockSpec((blk, hpg * hd), lambda b, h, i: (b * nq + i, h)),
                  pl.BlockSpec((seq, hpg * hd), lambda b, h, i: (b, hg + h)),
                  pl.BlockSpec((seq, hpg * hd), lambda b, h, i: (b, 2 * hg + h))],
        out_specs=pl.BlockSpec((blk, hpg * hd), lambda b, h, i: (b * nq + i, h)),
        scratch_shapes=[pltpu.VMEM((hpg, nq, hd), F32), pltpu.VMEM((hpg, seq, hd), BF16),
                        pltpu.VMEM((hpg, nq, hd, blk), BF16), pltpu.VMEM((hpg * (blk // LANES), nq, LANES), F32),
                        pltpu.VMEM((hpg * (blk // LANES), hd, LANES), F32),
                        pltpu.VMEM((hpg * (blk // LANES), blk, LANES), F32)],
        compiler_params=_params("arbitrary", "arbitrary", "arbitrary"),
    )(qkv, qkv, qkv)


PAGES_PER_STEP = 8


def _moba_sample_kernel(pt_ref, q_ref, kn_ref, vn_ref, *rest, n_pages, page, heads, n_past_blocks, scale):
    k_refs = rest[:PAGES_PER_STEP]
    v_refs = rest[PAGES_PER_STEP:2 * PAGES_PER_STEP]
    o_ref, sc_ref, gate_ref, acc_ref, l_ref = rest[2 * PAGES_PER_STEP:]
    ph = pl.program_id(1)
    s = pl.program_id(2)
    pages_per_block = MOBA_BLOCK // page
    q = q_ref[...]
    R, hd = q.shape
    cols = page * heads
    q16 = (q * scale).astype(BF16)
    lane = lax.broadcasted_iota(jnp.int32, (R, LANES), 1)

    def head_of(shape, dim):
        return lax.rem(lax.broadcasted_iota(jnp.int32, shape, dim), heads)

    @pl.when((ph == 0) & (s == 0))
    def _():
        gate_ref[...] = jnp.zeros_like(gate_ref)

    @pl.when(ph == 0)
    def _():
        for r in range(PAGES_PER_STEP):
            k3 = k_refs[r][...]
            sc_ref[s * PAGES_PER_STEP + r] = _nt_dot(q16, k3.reshape(cols, hd).astype(BF16))
            ksum = jnp.sum(k3, axis=0) * (1.0 / MOBA_BLOCK)
            g = jnp.sum(q * jnp.concatenate([ksum] * (R // heads), axis=0), axis=-1, keepdims=True)
            blk_id = (s * PAGES_PER_STEP + r) // pages_per_block
            gate_ref[...] += jnp.where(lane == blk_id, g, 0.0)

    @pl.when((ph == 1) & (s == 0))
    def _():
        sel = _block_rank_select(gate_ref[...], lane < n_past_blocks, lane, n_past_blocks, axis=1)
        same_head = head_of((R, cols), 1) == head_of((R, cols), 0)
        s_own = _nt_dot(q16, kn_ref[...].astype(BF16))
        r_i = lax.broadcasted_iota(jnp.int32, (R, R), 0)
        c_i = lax.broadcasted_iota(jnp.int32, (R, R), 1)
        r_h, c_h = head_of((R, R), 0), head_of((R, R), 1)
        s_own = jnp.where((r_h == c_h) & (c_i - c_h <= r_i - r_h), s_own, NEG_INF)
        mx = jnp.full((R, cols), NEG_INF, F32)
        for pg in range(n_pages):
            b_id = pg // pages_per_block
            mx = jnp.maximum(mx, jnp.where(sel[:, b_id:b_id + 1] & same_head, sc_ref[pg], NEG_INF))
        m = jnp.maximum(jnp.max(s_own, axis=-1, keepdims=True), jnp.max(mx, axis=-1, keepdims=True))
        p_own = jnp.exp(s_own - m)
        psum = jnp.zeros((R, cols), F32)
        for pg in range(n_pages):
            b_id = pg // pages_per_block
            p = jnp.where(sel[:, b_id:b_id + 1] & same_head, jnp.exp(sc_ref[pg] - m), 0.0)
            sc_ref[pg] = p
            psum = psum + p
        l = jnp.sum(p_own, axis=-1, keepdims=True) + jnp.sum(psum, axis=-1, keepdims=True)
        l_ref[...] = jnp.broadcast_to(l, l_ref.shape)
        acc_ref[...] = jnp.dot(p_own.astype(BF16), vn_ref[...].astype(BF16), preferred_element_type=F32)

    @pl.when(ph == 1)
    def _():
        acc = acc_ref[...]
        for r in range(PAGES_PER_STEP):
            p = sc_ref[s * PAGES_PER_STEP + r]
            acc = acc + jnp.dot(p.astype(BF16), v_refs[r][...].reshape(cols, hd).astype(BF16),
                                preferred_element_type=F32)
        acc_ref[...] = acc

    @pl.when((ph == 1) & (s == pl.num_programs(2) - 1))
    def _():
        o_ref[...] = acc_ref[...] / l_ref[:, 0:1]


def moba_sample(qkv, cache_k, cache_v, page_table, layer_j, n_seq, t_new, heads, hd):
    d = heads * hd
    n_pool, n_a, page = cache_k.shape[0], cache_k.shape[1], cache_k.shape[2]
    n_pages = page_table.shape[1]
    past = n_pages * page
    assert past % MOBA_BLOCK == 0 and MOBA_BLOCK % page == 0 and t_new <= MOBA_BLOCK
    assert n_pages % PAGES_PER_STEP == 0
    n_steps = n_pages // PAGES_PER_STEP
    n_past_blocks = past // MOBA_BLOCK
    assert MOBA_TOPK <= n_past_blocks <= 128
    R = heads * t_new
    qkv3 = qkv.reshape(n_seq, t_new, 3 * d)
    q, k_new, v_new = (qkv3[:, :, c * d:(c + 1) * d].reshape(n_seq, R, hd) for c in range(3))
    pt = page_table.reshape(-1).astype(jnp.int32)

    def k_map(r):
        return lambda b, ph, s, pt: (pt[b * n_pages + jnp.where(ph == 0, s, n_steps - 1) * PAGES_PER_STEP + r],
                                     layer_j, 0, 0, 0)

    def v_map(r):
        return lambda b, ph, s, pt: (pt[b * n_pages + jnp.where(ph == 0, 0, s) * PAGES_PER_STEP + r],
                                     layer_j, 0, 0, 0)

    seq_blk = pl.BlockSpec((None, R, hd), lambda b, ph, s, pt: (b, 0, 0))
    page_blk = lambda index_map: pl.BlockSpec((None, None, page, heads, hd), index_map)
    o = pl.pallas_call(
        functools.partial(_moba_sample_kernel, n_pages=n_pages, page=page, heads=heads,
                          n_past_blocks=n_past_blocks, scale=hd ** -0.5),
        out_shape=jax.ShapeDtypeStruct((n_seq, R, hd), F32),
        grid_spec=pltpu.PrefetchScalarGridSpec(
            num_scalar_prefetch=1, grid=(n_seq, 2, n_steps),
            in_specs=[seq_blk, seq_blk, seq_blk]
                     + [page_blk(k_map(r)) for r in range(PAGES_PER_STEP)]
                     + [page_blk(v_map(r)) for r in range(PAGES_PER_STEP)],
            out_specs=seq_blk,
            scratch_shapes=[pltpu.VMEM((n_pages, R, page * heads), F32), pltpu.VMEM((R, LANES), F32),
                            pltpu.VMEM((R, hd), F32), pltpu.VMEM((R, LANES), F32)]),
        compiler_params=_params("arbitrary", "arbitrary", "arbitrary"),
    )(pt, q, k_new, v_new, *([cache_k] * PAGES_PER_STEP), *([cache_v] * PAGES_PER_STEP))
    return o.reshape(n_seq * t_new, d)


def _softplus(x):
    return jnp.maximum(x, 0.0) + jnp.log1p(jnp.exp(-jnp.abs(x)))


def _lru_gates(xc, wra_ref, bra, wix_ref, bix, sp, n, bw):
    cs = slice(n * bw, (n + 1) * bw)
    xcb = xc[:, cs]
    xcb16 = xcb.astype(BF16)
    r = jax.nn.sigmoid(jnp.dot(xcb16, wra_ref[n].astype(BF16), preferred_element_type=F32) + bra[:, cs])
    ig = jax.nn.sigmoid(jnp.dot(xcb16, wix_ref[n].astype(BF16), preferred_element_type=F32) + bix[:, cs])
    log_a = -LRU_C * r * sp[:, cs]
    a = jnp.exp(log_a)
    b = jnp.sqrt(jnp.tanh(-log_a) * (1.0 + a * a)) * (ig * xcb)
    return a, b


def _lru_prompt_kernel(yb_ref, xr_ref, cw_ref, cb_ref, wra_ref, bra_ref, wix_ref, bix_ref, lam_ref,
                       z_ref, cs_ref, hl_ref, xbuf, hcar, *, tc, n_lru_blocks, bw, conv_w):
    i = pl.program_id(1)

    @pl.when(i == 0)
    def _():
        xbuf[0:8, :] = jnp.zeros((8, xbuf.shape[1]), F32)
        hcar[...] = jnp.zeros_like(hcar)

    xr = xr_ref[...]
    xbuf[8:8 + tc, :] = xr
    cw = cw_ref[...]
    xc = cb_ref[...] + xr * cw[conv_w - 1:conv_w, :]
    for s in range(1, conv_w):
        xc = xc + xbuf[8 - s:8 - s + tc, :] * cw[conv_w - 1 - s:conv_w - s, :]
    xbuf[0:8, :] = xbuf[tc:tc + 8, :]
    sp = _softplus(-lam_ref[...])
    bra, bix = bra_ref[...], bix_ref[...]
    row = lax.broadcasted_iota(jnp.int32, (tc, bw), 0)
    for n in range(n_lru_blocks):
        cs = slice(n * bw, (n + 1) * bw)
        a, b = _lru_gates(xc, wra_ref, bra, wix_ref, bix, sp, n, bw)
        sh = 1
        while sh < tc:
            keep = row >= sh
            a_prev = jnp.where(keep, pltpu.roll(a, sh, 0), 1.0)
            b_prev = jnp.where(keep, pltpu.roll(b, sh, 0), 0.0)
            b = a * b_prev + b
            a = a * a_prev
            sh *= 2
        h = a * hcar[:, cs] + b
        hcar[:, cs] = h[tc - 1:tc, :]
        z_ref[:, cs] = (h * jax.nn.gelu(yb_ref[:, cs])).astype(z_ref.dtype)
    cs_ref[...] = xbuf[8 - (conv_w - 1):8, :]
    hl_ref[...] = hcar[...]


def lru_prompt(br, n_seq, seq, conv_w, conv_b, w_ra, b_ra, w_ix, b_ix, lam, tc=256):
    w = br.shape[1] // 2
    cw = conv_w.shape[0]
    nb, bw = w_ra.shape[0], w_ra.shape[1]
    assert seq % tc == 0 and cw - 1 <= 8 <= tc
    nt = seq // tc
    vec = pl.BlockSpec((1, w), lambda b, i: (0, 0))
    wblk = pl.BlockSpec((nb, bw, bw), lambda b, i: (0, 0, 0))
    return pl.pallas_call(
        functools.partial(_lru_prompt_kernel, tc=tc, n_lru_blocks=nb, bw=bw, conv_w=cw),
        out_shape=(jax.ShapeDtypeStruct((n_seq * seq, w), BF16),
                   jax.ShapeDtypeStruct((n_seq, cw - 1, w), F32),
                   jax.ShapeDtypeStruct((n_seq, 1, w), F32)),
        grid=(n_seq, nt),
        in_specs=[pl.BlockSpec((tc, w), lambda b, i: (b * nt + i, 0)),
                  pl.BlockSpec((tc, w), lambda b, i: (b * nt + i, 1)),
                  pl.BlockSpec((cw, w), lambda b, i: (0, 0)), vec, wblk, vec, wblk, vec, vec],
        out_specs=(pl.BlockSpec((tc, w), lambda b, i: (b * nt + i, 0)),
                   pl.BlockSpec((None, cw - 1, w), lambda b, i: (b, 0, 0)),
                   pl.BlockSpec((None, 1, w), lambda b, i: (b, 0, 0))),
        scratch_shapes=[pltpu.VMEM((8 + tc, w), F32), pltpu.VMEM((1, w), F32)],
        compiler_params=_params("arbitrary", "arbitrary"),
    )(br, br, conv_w, conv_b.reshape(1, w), w_ra, b_ra.reshape(1, w), w_ix, b_ix.reshape(1, w), lam.reshape(1, w))


def _lru_sample_kernel(br_ref, c0_ref, h0_ref, cw_ref, cb_ref, wra_ref, bra_ref, wix_ref, bix_ref, lam_ref,
                       z_ref, cs_ref, hl_ref, *, t_new, n_lru_blocks, bw, conv_w, w):
    br = br_ref[...]
    yb, xr = br[:, :w], br[:, w:]
    xpad = jnp.concatenate([c0_ref[...], xr], axis=0)
    cw = cw_ref[...]
    xc = cb_ref[...] + xpad[0:t_new, :] * cw[0:1, :]
    for i in range(1, conv_w):
        xc = xc + xpad[i:i + t_new, :] * cw[i:i + 1, :]
    sp = _softplus(-lam_ref[...])
    bra, bix = bra_ref[...], bix_ref[...]
    gelu_y = jax.nn.gelu(yb)
    for n in range(n_lru_blocks):
        cs = slice(n * bw, (n + 1) * bw)
        a, b = _lru_gates(xc, wra_ref, bra, wix_ref, bix, sp, n, bw)
        h = h0_ref[:, cs]
        for t in range(t_new):
            h = a[t:t + 1, :] * h + b[t:t + 1, :]
            z_ref[t:t + 1, cs] = h * gelu_y[t:t + 1, cs]
        hl_ref[:, cs] = h
    cs_ref[...] = xpad[t_new:t_new + conv_w - 1, :]


def lru_sample(br, conv0, h0, n_seq, t_new, conv_w, conv_b, w_ra, b_ra, w_ix, b_ix, lam):
    w = br.shape[1] // 2
    cw = conv_w.shape[0]
    nb, bw = w_ra.shape[0], w_ra.shape[1]
    vec = pl.BlockSpec((1, w), lambda b: (0, 0))
    wblk = pl.BlockSpec((nb, bw, bw), lambda b: (0, 0, 0))
    per_seq = lambda rows, cols: pl.BlockSpec((None, rows, cols), lambda b: (b, 0, 0))
    z, cs, hl = pl.pallas_call(
        functools.partial(_lru_sample_kernel, t_new=t_new, n_lru_blocks=nb, bw=bw, conv_w=cw, w=w),
        out_shape=(jax.ShapeDtypeStruct((n_seq, t_new, w), F32),
                   jax.ShapeDtypeStruct((n_seq, cw - 1, w), F32),
                   jax.ShapeDtypeStruct((n_seq, 1, w), F32)),
        grid=(n_seq,),
        in_specs=[per_seq(t_new, 2 * w), per_seq(cw - 1, w), per_seq(1, w),
                  pl.BlockSpec((cw, w), lambda b: (0, 0)), vec, wblk, vec, wblk, vec, vec],
        out_specs=(per_seq(t_new, w), per_seq(cw - 1, w), per_seq(1, w)),
        compiler_params=_params("arbitrary"),
    )(br.reshape(n_seq, t_new, 2 * w), conv0, h0.reshape(n_seq, 1, w), conv_w, conv_b.reshape(1, w),
      w_ra, b_ra.reshape(1, w), w_ix, b_ix.reshape(1, w), lam.reshape(1, w))
    return z.reshape(n_seq * t_new, w), cs, hl


def _rope(x, cos, sin, half):
    x1, x2 = x[:, :half], x[:, half:]
    return jnp.concatenate([x1 * cos - x2 * sin, x2 * cos + x1 * sin], axis=-1)


def _group_norm_gate(o, g, gn_g, gn_b):
    mu = jnp.mean(o, axis=-1, keepdims=True)
    d = o - mu
    var = jnp.mean(d * d, axis=-1, keepdims=True)
    on = d * lax.rsqrt(var + LN_EPS) * gn_g + gn_b
    return (g * jax.nn.sigmoid(g)) * on


def _ret_prompt_kernel(q_ref, k_ref, v_ref, g_ref, cos_ref, sin_ref, dm_ref, cd_ref, kd_ref, chd_ref,
                       gng_ref, gnb_ref, z_ref, s_ref, S, *, kscale, half):
    c = pl.program_id(2)

    @pl.when(c == 0)
    def _():
        S[...] = jnp.zeros_like(S)

    cos, sin = cos_ref[...], sin_ref[...]
    q = _rope(q_ref[...], cos, sin, half)
    k = _rope(k_ref[...] * kscale, cos, sin, half)
    qb, kb, vb = q.astype(BF16), k.astype(BF16), v_ref[...].astype(BF16)
    inner = _nt_dot(qb, kb) * dm_ref[...]
    s_old = S[...]
    o = (jnp.dot(inner.astype(BF16), vb, preferred_element_type=F32)
         + jnp.dot(qb, s_old.astype(BF16), preferred_element_type=F32) * cd_ref[...])
    kdt = (k * kd_ref[...]).T.astype(BF16)
    S[...] = chd_ref[...] * s_old + jnp.dot(kdt, vb, preferred_element_type=F32)
    z_ref[...] = _group_norm_gate(o, g_ref[...], gng_ref[...], gnb_ref[...]).astype(z_ref.dtype)

    @pl.when(c == pl.num_programs(2) - 1)
    def _():
        s_ref[...] = S[...]


def _decay_tables(heads, c):
    f = np.float32
    lg = np.log1p(-np.exp2(f(-5.0) - np.arange(heads, dtype=f))).astype(f)
    idx = np.arange(c, dtype=f)
    diff = idx[:, None] - idx[None, :]
    dmask = np.where(diff >= 0, np.exp(np.maximum(diff, f(0.0))[None] * lg[:, None, None]), f(0.0)).astype(f)
    cross = np.exp((idx + f(1.0))[None, :] * lg[:, None])[..., None].astype(f)
    kdec = np.exp((f(c) - f(1.0) - idx)[None, :] * lg[:, None])[..., None].astype(f)
    chunk = np.exp(f(c) * lg)[:, None, None].astype(f)
    return dmask, cross, kdec, chunk


def _rope_tables(pos0, t, half):
    f = np.float32
    inv = (f(ROPE_BASE) ** (-np.arange(half, dtype=f) / f(half))).astype(f)
    ang = (np.arange(pos0, pos0 + t).astype(f)[:, None] * inv[None, :]).astype(f)
    return np.cos(ang).astype(f), np.sin(ang).astype(f)


def ret_prompt(proj, n_seq, seq, heads, kdim, vdim, gn_g, gn_b, tc=256):
    assert seq % tc == 0 and vdim % kdim == 0
    nc = seq // tc
    hk, hv = heads * kdim, heads * vdim
    half = kdim // 2
    cos, sin = _rope_tables(0, seq, half)
    dmask, cross, kdec, chunk = _decay_tables(heads, tc)
    v0, g0 = 2 * hk // vdim, (2 * hk + hv) // vdim
    row = lambda b, h, c: b * nc + c
    per_head = lambda r, cdim: pl.BlockSpec((None, r, cdim), lambda b, h, c: (h, 0, 0))
    return pl.pallas_call(
        functools.partial(_ret_prompt_kernel, kscale=kdim ** -0.5, half=half),
        out_shape=(jax.ShapeDtypeStruct((n_seq * seq, hv), BF16),
                   jax.ShapeDtypeStruct((n_seq, heads, kdim, vdim), F32)),
        grid=(n_seq, heads, nc),
        in_specs=[pl.BlockSpec((tc, kdim), lambda b, h, c: (row(b, h, c), h)),
                  pl.BlockSpec((tc, kdim), lambda b, h, c: (row(b, h, c), heads + h)),
                  pl.BlockSpec((tc, vdim), lambda b, h, c: (row(b, h, c), v0 + h)),
                  pl.BlockSpec((tc, vdim), lambda b, h, c: (row(b, h, c), g0 + h)),
                  pl.BlockSpec((tc, half), lambda b, h, c: (c, 0)),
                  pl.BlockSpec((tc, half), lambda b, h, c: (c, 0)),
                  per_head(tc, tc), per_head(tc, 1), per_head(tc, 1), per_head(1, 1),
                  pl.BlockSpec((1, vdim), lambda b, h, c: (0, h)),
                  pl.BlockSpec((1, vdim), lambda b, h, c: (0, h))],
        out_specs=(pl.BlockSpec((tc, vdim), lambda b, h, c: (row(b, h, c), h)),
                   pl.BlockSpec((None, None, kdim, vdim), lambda b, h, c: (b, h, 0, 0))),
        scratch_shapes=[pltpu.VMEM((kdim, vdim), F32)],
        compiler_params=_params("arbitrary", "arbitrary", "arbitrary"),
    )(proj, proj, proj, proj, cos, sin, dmask, cross, kdec, chunk, gn_g.reshape(1, hv), gn_b.reshape(1, hv))


def _ret_sample_kernel(p_ref, kt_ref, s0_ref, cos_ref, sin_ref, cost_ref, sint_ref, dm_ref, cd_ref, kd_ref, chd_ref,
                       gng_ref, gnb_ref, z_ref, s_ref, *, heads, kdim, vdim, t_new, kscale):
    half = kdim // 2
    hk, hv = heads * kdim, heads * vdim
    p = p_ref[...]
    cos, sin = cos_ref[...], sin_ref[...]
    cost, sint = cost_ref[...], sint_ref[...]
    for h in range(heads):
        q = _rope(p[:, h * kdim:(h + 1) * kdim], cos, sin, half)
        k = _rope(p[:, hk + h * kdim:hk + (h + 1) * kdim] * kscale, cos, sin, half)
        v = p[:, 2 * hk + h * vdim:2 * hk + (h + 1) * vdim]
        g = p[:, 2 * hk + hv + h * vdim:2 * hk + hv + (h + 1) * vdim]
        kt = kt_ref[h] * kscale
        k1, k2 = kt[:half, :], kt[half:, :]
        kt = jnp.concatenate([k1 * cost - k2 * sint, k2 * cost + k1 * sint], axis=0) * kd_ref[h]
        s0 = s0_ref[h]
        dm = dm_ref[h]
        o = jnp.dot(q, s0, preferred_element_type=F32, precision=HIGHEST) * cd_ref[h]
        s_new = chd_ref[h] * s0
        for m in range(t_new):
            inner_m = jnp.sum(q * k[m:m + 1, :], axis=-1, keepdims=True) * dm[:, m:m + 1]
            o = o + inner_m * v[m:m + 1, :]
            s_new = s_new + kt[:, m:m + 1] * v[m:m + 1, :]
        s_ref[h] = s_new
        z_ref[:, h * vdim:(h + 1) * vdim] = _group_norm_gate(
            o, g, gng_ref[:, h * vdim:(h + 1) * vdim], gnb_ref[:, h * vdim:(h + 1) * vdim])


def ret_sample(proj, s0, pos0, n_seq, t_new, heads, kdim, vdim, gn_g, gn_b):
    hk, hv = heads * kdim, heads * vdim
    half = kdim // 2
    width = 2 * hk + 2 * hv
    cos, sin = _rope_tables(pos0, t_new, half)
    dmask, cross, kdec, chunk = _decay_tables(heads, t_new)
    proj3 = proj.reshape(n_seq, t_new, width)
    kt = proj3[:, :, hk:2 * hk].reshape(n_seq, t_new, heads, kdim).transpose(0, 2, 3, 1)
    full = lambda *shape: pl.BlockSpec(shape, lambda b: (0,) * len(shape))
    z, s = pl.pallas_call(
        functools.partial(_ret_sample_kernel, heads=heads, kdim=kdim, vdim=vdim, t_new=t_new, kscale=kdim ** -0.5),
        out_shape=(jax.ShapeDtypeStruct((n_seq, t_new, hv), F32),
                   jax.ShapeDtypeStruct((n_seq, heads, kdim, vdim), F32)),
        grid=(n_seq,),
        in_specs=[pl.BlockSpec((None, t_new, width), lambda b: (b, 0, 0)),
                  pl.BlockSpec((None, heads, kdim, t_new), lambda b: (b, 0, 0, 0)),
                  pl.BlockSpec((None, heads, kdim, vdim), lambda b: (b, 0, 0, 0)),
                  full(t_new, half), full(t_new, half), full(half, t_new), full(half, t_new),
                  full(heads, t_new, t_new), full(heads, t_new, 1), full(heads, 1, t_new), full(heads, 1, 1),
                  full(1, hv), full(1, hv)],
        out_specs=(pl.BlockSpec((None, t_new, hv), lambda b: (b, 0, 0)),
                   pl.BlockSpec((None, heads, kdim, vdim), lambda b: (b, 0, 0, 0))),
        compiler_params=_params("arbitrary"),
    )(proj3, kt, s0, cos, sin, cos.T, sin.T, dmask, cross, kdec.transpose(0, 2, 1), chunk,
      gn_g.reshape(1, hv), gn_b.reshape(1, hv))
    return z.reshape(n_seq * t_new, hv), s


def kernel(x_prompt, x_sample, cache_k, cache_v, state_lru_conv, state_lru_h, state_ret, page_table, c_prompt, c_sample, w_ada, b_ada, ln_g, ln_b, a_w_qkv, a_w_o, lru_w_in, lru_b_in, lru_conv_w, lru_conv_b, lru_w_ra, lru_b_ra, lru_w_ix, lru_b_ix, lru_lambda, lru_w_out, ret_w_in, ret_gn_g, ret_gn_b, ret_w_o, moe_w_router, moe_b_router, moe_w1, moe_b1, moe_w2, moe_b2):
    depth, d = w_ada.shape[0], w_ada.shape[1]
    n_mixers = 3
    alpha = (2 * depth) ** 0.25
    heads, hd = cache_k.shape[3], cache_k.shape[4]
    r_heads, r_kdim, r_vdim = state_ret.shape[2], state_ret.shape[3], state_ret.shape[4]
    n_e = moe_w_router.shape[2]
    past_len = page_table.shape[1] * cache_k.shape[2]

    bp, bs = c_prompt.shape[0], c_sample.shape[0]
    n_c = bp + bs
    c_rows = 16 * (-(-n_c // 16))
    c_all = jnp.pad(jnp.concatenate([c_prompt, c_sample], axis=0), ((0, c_rows - n_c), (0, 0)))
    mods = [matmul(c_all, w_ada[l], b_ada[l], pre_silu=True, tn=1024) for l in range(depth)]

    class Group:
        def __init__(self, x, seq_lo, sample):
            self.sample = sample
            self.n_seq, self.t, _ = x.shape
            self.n = self.n_seq * self.t
            self.tm = min(ROW_TILE, self.n)
            self.ms = [ModSource(mods[l][seq_lo:seq_lo + self.n_seq], self.t, self.tm, d) for l in range(depth)]
            self.x2 = x.reshape(self.n, d)
            self.u = modulate(self.x2, self.ms[0], 1, 0)
            self.ks, self.vs, self.convs, self.hs, self.ss = [], [], [], [], []
            self.k_all = self.v_all = None

        def outputs(self):
            if self.k_all is None:
                lead = (self.n_seq, self.t, heads, hd)
                self.k_all = jnp.stack([k.reshape(lead) for k in self.ks], 1)
                self.v_all = jnp.stack([v.reshape(lead) for v in self.vs], 1)
            return (self.x2.reshape(self.n_seq, self.t, d), self.k_all, self.v_all,
                    jnp.stack(self.convs, 1), jnp.stack(self.hs, 1), jnp.stack(self.ss, 1))

    def mixer(g, layer):
        kind, j = layer % n_mixers, layer // n_mixers
        u, sample, n_seq, t = g.u, g.sample, g.n_seq, g.t
        if kind == 0:
            if sample:
                qkv = matmul(u, a_w_qkv[j])
                o = moba_sample(qkv, cache_k, cache_v, page_table, j, n_seq, t, heads, hd)
                k3, v3 = qkv[:, d:2 * d], qkv[:, 2 * d:]
                g.ks.append(k3.reshape(n_seq, t, heads, hd))
                g.vs.append(v3.reshape(n_seq, t, heads, hd))
            elif 0 < j == a_w_qkv.shape[0] - 1:
                qkv, g.k_all, g.v_all = qkv_project(u, a_w_qkv[j], heads, hd, n_seq, t, g.ks, g.vs)
                o = moba_prompt(qkv, n_seq, t, heads, hd)
            else:
                qkv, k3, v3 = qkv_project(u, a_w_qkv[j], heads, hd, n_seq, t)
                o = moba_prompt(qkv, n_seq, t, heads, hd)
                g.ks.append(k3)
                g.vs.append(v3)
            return matmul(o, a_w_o[j])
        if kind == 1:
            br = matmul(u, lru_w_in[j], lru_b_in[j])
            lru_args = (lru_conv_w[j], lru_conv_b[j], lru_w_ra[j], lru_b_ra[j], lru_w_ix[j], lru_b_ix[j],
                        lru_lambda[j])
            if sample:
                z, cs, hl = lru_sample(br, state_lru_conv[:, j], state_lru_h[:, j], n_seq, t, *lru_args)
            else:
                z, cs, hl = lru_prompt(br, n_seq, t, *lru_args)
            g.convs.append(cs)
            g.hs.append(hl.reshape(n_seq, -1))
            return matmul(z, lru_w_out[j])
        proj = matmul(u, ret_w_in[j])
        if sample:
            z, s_fin = ret_sample(proj, state_ret[:, j], past_len, n_seq, t, r_heads, r_kdim, r_vdim,
                                  ret_gn_g[j], ret_gn_b[j])
        else:
            z, s_fin = ret_prompt(proj, n_seq, t, r_heads, r_kdim, r_vdim, ret_gn_g[j], ret_gn_b[j])
        g.ss.append(s_fin)
        return matmul(z, ret_w_o[j])

    groups = [Group(x_prompt, 0, False), Group(x_sample, bp, True)]
    tile = MOE_TILE
    n_assign = sum(g.n for g in groups) * MOE_TOPK
    n_blk = -(-n_assign // tile) + n_e
    for layer in range(depth):
        counts = [jnp.zeros((1, n_e), jnp.int32)]
        routed = []
        for g in groups:
            y = mixer(g, layer)
            g.x2, u3, top_e, gates, rank, cnt = ln_router(
                g.x2, y, counts[-1], g.ms[layer], 2, 4, 3, ln_g[layer, 0], ln_b[layer, 0],
                moe_w_router[layer], moe_b_router[layer], alpha)
            counts.append(cnt)
            routed.append((u3, top_e, gates, rank))
        start, blk_e, pad_bounds = route_tables(counts[-1].reshape(n_e), counts[1].reshape(n_e), tile, n_blk)
        xs, inv, first_assign = None, None, []
        for g, (u3, top_e, gates, rank) in zip(groups, routed):
            pos = (jnp.take(start, top_e) + rank).reshape(g.n * MOE_TOPK)
            first_assign.append(sum(h.n for h in groups[:len(first_assign)]) * MOE_TOPK)
            if xs is None:
                xs, inv = dispatch(u3, pos, pad_bounds, None, n_blk * tile, g.tm, n_e, tile, n_assign)
            else:
                xs = dispatch(u3, pos, pad_bounds, xs, n_blk * tile, g.tm, n_e, tile, n_assign)
                inv = inv.at[pos].set(first_assign[-1] + jnp.arange(g.n * MOE_TOPK, dtype=jnp.int32))
        ytok = experts(xs, blk_e, inv, moe_w1[layer], moe_b1[layer], moe_w2[layer], moe_b2[layer], tile)
        for g, (u3, top_e, gates, rank), a0 in zip(groups, routed, first_assign):
            nxt = (g.ms[layer + 1], 1, 0) if layer + 1 < depth else ()
            g.x2, g.u = ln_combine(g.x2, gates, ytok, a0, g.ms[layer], 5, ln_g[layer, 1], ln_b[layer, 1], alpha,
                                   *nxt)

    y_p, k_p, v_p, conv_p, h_p, s_p = groups[0].outputs()
    y_s, k_s, v_s, conv_s, h_s, s_s = groups[1].outputs()
    return (y_p, y_s, k_p, v_p, conv_p, h_p, s_p, k_s, v_s, conv_s, h_s, s_s)
```

```python
import functools

import numpy as np
import jax
import jax.numpy as jnp
from jax import lax
from jax.experimental import pallas as pl
from jax.experimental.pallas import tpu as pltpu

F32 = jnp.float32
BF16 = jnp.bfloat16
HIGHEST = lax.Precision.HIGHEST
NEG_INF = float("-inf")

MOBA_BLOCK = 256
MOBA_TOPK = 3
MOE_TOPK = 4
LRU_C = 8.0
ROPE_BASE = 10000.0
SWIGLU_LIMIT = 7.0
SWIGLU_ALPHA = 1.702
LN_EPS = 1e-5

LANES = 128
DMA_UNROLL = 8
MOE_TILE = 256
ROW_TILE = 256
V7X_VMEM_BYTES = 64 * 2**20
VMEM_LIMIT = V7X_VMEM_BYTES - 8 * 2**20


def _params(*sem):
    return pltpu.CompilerParams(dimension_semantics=sem, vmem_limit_bytes=VMEM_LIMIT)


def _nt_dot(a, b, **kw):
    return lax.dot_general(a, b, (((1,), (1,)), ((), ())), preferred_element_type=F32, **kw)


def _mm_kernel(*refs, has_bias, pre_silu):
    if has_bias:
        x_ref, w_ref, b_ref, o_ref, wbf_ref = refs
    else:
        x_ref, w_ref, o_ref, wbf_ref = refs

    @pl.when(pl.program_id(1) == 0)
    def _():
        wbf_ref[...] = w_ref[...].astype(BF16)

    x = x_ref[...]
    if pre_silu:
        x = x * jax.nn.sigmoid(x)
    acc = jnp.dot(x.astype(BF16), wbf_ref[...], preferred_element_type=F32)
    if has_bias:
        acc = acc + b_ref[...]
    o_ref[...] = acc


def matmul(x, w, b=None, *, layer=None, pre_silu=False, tm=1024, tn=1024):
    M, K = x.shape
    N = w.shape[-1]
    tm, tn = min(tm, M), min(tn, N)
    assert M % tm == 0 and N % tn == 0, (M, N, tm, tn)
    if layer is None:
        w_spec = pl.BlockSpec((K, tn), lambda j, i: (0, j))
        b_spec = pl.BlockSpec((1, tn), lambda j, i: (0, j))
        b = None if b is None else b.reshape(1, N)
    else:
        w_spec = pl.BlockSpec((None, K, tn), lambda j, i: (layer, 0, j))
        b_spec = pl.BlockSpec((None, 1, tn), lambda j, i: (layer, 0, j))
        b = None if b is None else b.reshape(b.shape[0], 1, N)
    in_specs = [pl.BlockSpec((tm, K), lambda j, i: (i, 0)), w_spec]
    args = [x, w]
    if b is not None:
        in_specs.append(b_spec)
        args.append(b)
    return pl.pallas_call(
        functools.partial(_mm_kernel, has_bias=b is not None, pre_silu=pre_silu),
        out_shape=jax.ShapeDtypeStruct((M, N), F32),
        grid=(N // tn, M // tm),
        in_specs=in_specs,
        out_specs=pl.BlockSpec((tm, tn), lambda j, i: (i, j)),
        scratch_shapes=[pltpu.VMEM((K, tn), BF16)],
        compiler_params=_params("arbitrary", "arbitrary"),
    )(*args)


def _qkv_kernel(x_ref, w_ref, *rest, d, n_prev):
    prev_k, prev_v = rest[:n_prev], rest[n_prev:2 * n_prev]
    qkv_ref, k_ref, v_ref, wbf_ref = rest[2 * n_prev:]

    @pl.when(pl.program_id(0) == 0)
    def _():
        wbf_ref[...] = w_ref[...].astype(BF16)

    acc = jnp.dot(x_ref[...].astype(BF16), wbf_ref[...], preferred_element_type=F32)
    qkv_ref[...] = acc
    leaf = k_ref.shape[-3:]
    k_new, v_new = acc[:, d:2 * d].reshape(leaf), acc[:, 2 * d:].reshape(leaf)
    if n_prev == 0:
        k_ref[...] = k_new
        v_ref[...] = v_new
    else:
        for p in range(n_prev):
            k_ref[p] = prev_k[p][...]
            v_ref[p] = prev_v[p][...]
        k_ref[n_prev] = k_new
        v_ref[n_prev] = v_new


def qkv_project(u, w, heads, hd, n_seq, seq, prev_k=(), prev_v=(), tm=256):
    n, d = u.shape
    assert n == n_seq * seq and seq % tm == 0 and w.shape == (d, 3 * d) and heads * hd == d
    n_prev = len(prev_k)
    tps = seq // tm
    tok_blk = pl.BlockSpec((tm, heads, hd), lambda i: (i, 0, 0))
    if n_prev == 0:
        leaf, leaf_blk = jax.ShapeDtypeStruct((n, heads, hd), F32), tok_blk
    else:
        leaf = jax.ShapeDtypeStruct((n_seq, n_prev + 1, seq, heads, hd), F32)
        leaf_blk = pl.BlockSpec((None, n_prev + 1, tm, heads, hd), lambda i: (i // tps, 0, i % tps, 0, 0))
    return pl.pallas_call(
        functools.partial(_qkv_kernel, d=d, n_prev=n_prev),
        out_shape=(jax.ShapeDtypeStruct((n, 3 * d), F32), leaf, leaf),
        grid=(n // tm,),
        in_specs=[pl.BlockSpec((tm, d), lambda i: (i, 0)),
                  pl.BlockSpec((d, 3 * d), lambda i: (0, 0), pipeline_mode=pl.Buffered(1))]
                 + [tok_blk] * (2 * n_prev),
        out_specs=(pl.BlockSpec((tm, 3 * d), lambda i: (i, 0)), leaf_blk, leaf_blk),
        scratch_shapes=[pltpu.VMEM((d, 3 * d), BF16)],
        compiler_params=_params("arbitrary"),
    )(u, w, *prev_k, *prev_v)


class ModSource:
    def __init__(self, mod, rows_per_seq, tm, d):
        self.d = d
        self.tm = tm
        if rows_per_seq % tm == 0:
            self.per_seq = True
            self.tiles_per_seq = rows_per_seq // tm
            self.array = mod.reshape(mod.shape[0], 1, mod.shape[1])
        else:
            self.per_seq = False
            self.array = jnp.repeat(mod, rows_per_seq, axis=0)

    def spec(self, chunk):
        if self.per_seq:
            tps = self.tiles_per_seq
            return pl.BlockSpec((None, 1, self.d), lambda i, *_: (i // tps, 0, chunk))
        return pl.BlockSpec((self.tm, self.d), lambda i, *_: (i, chunk))


def _layer_norm(z, g, b):
    mu = jnp.mean(z, axis=-1, keepdims=True)
    d = z - mu
    var = jnp.mean(d * d, axis=-1, keepdims=True)
    return d * lax.rsqrt(var + LN_EPS) * g + b


def _router(u, wr, br, e_ref, gt_ref, rk_ref, cnt_ref, carry):
    logits = jnp.dot(u, wr, preferred_element_type=F32, precision=HIGHEST) + br
    tm, n_e = logits.shape
    lane = lax.broadcasted_iota(jnp.int32, logits.shape, 1).astype(F32)
    slot = lax.broadcasted_iota(jnp.int32, (tm, MOE_TOPK), 1)
    idx_out = jnp.zeros((tm, MOE_TOPK), F32)
    val_out = jnp.zeros((tm, MOE_TOPK), F32)
    cur = logits
    top = None
    picks = []
    for k in range(MOE_TOPK):
        m = jnp.max(cur, axis=-1, keepdims=True)
        idx = jnp.min(jnp.where(cur == m, lane, float(n_e)), axis=-1, keepdims=True)
        if top is None:
            top = m
        idx_out = jnp.where(slot == k, idx, idx_out)
        val_out = jnp.where(slot == k, m, val_out)
        picks.append(lane == idx)
        cur = jnp.where(picks[-1], NEG_INF, cur)
    ex = jnp.exp(val_out - top)
    gt_ref[...] = ex / jnp.sum(ex, axis=-1, keepdims=True)
    e_ref[...] = idx_out.astype(jnp.int32)
    hot = sum(jnp.where(pk, 1.0, 0.0) for pk in picks)
    r_id = lax.broadcasted_iota(jnp.int32, (tm, tm), 0)
    c_id = lax.broadcasted_iota(jnp.int32, (tm, tm), 1)
    tri = jnp.where(c_id < r_id, 1.0, 0.0).astype(BF16)
    before = jnp.dot(tri, hot.astype(BF16), preferred_element_type=F32) + carry[...]
    rk_out = jnp.zeros((tm, MOE_TOPK), F32)
    for k in range(MOE_TOPK):
        rk = jnp.sum(jnp.where(picks[k], before, 0.0), axis=-1, keepdims=True)
        rk_out = jnp.where(slot == k, rk, rk_out)
    rk_ref[...] = rk_out.astype(jnp.int32)
    carry[...] += jnp.sum(hot, axis=0, keepdims=True)
    cnt_ref[...] = carry[...].astype(jnp.int32)


def _mod_kernel(x_ref, sc_ref, sh_ref, u_ref):
    u_ref[...] = (x_ref[...] * (1.0 + sc_ref[...]) + sh_ref[...]).astype(u_ref.dtype)


def modulate(x, ms, sc_chunk, sh_chunk):
    n, d = x.shape
    tm = ms.tm
    row = pl.BlockSpec((tm, d), lambda i: (i, 0))
    return pl.pallas_call(
        _mod_kernel,
        out_shape=jax.ShapeDtypeStruct((n, d), BF16),
        grid=(n // tm,),
        in_specs=[row, ms.spec(sc_chunk), ms.spec(sh_chunk)],
        out_specs=row,
        compiler_params=_params("arbitrary"),
    )(x, ms.array, ms.array)


def _ln_router_kernel(x_ref, y_ref, cnt0_ref, g_ref, lng_ref, lnb_ref, sc_ref, sh_ref, wr_ref, br_ref,
                      xo_ref, u3_ref, e_ref, gt_ref, rk_ref, cnt_ref, carry, *, alpha):
    @pl.when(pl.program_id(0) == 0)
    def _():
        carry[...] = cnt0_ref[...].astype(F32)

    xn = _layer_norm(alpha * x_ref[...] + g_ref[...] * y_ref[...], lng_ref[...], lnb_ref[...])
    xo_ref[...] = xn
    u = xn * (1.0 + sc_ref[...]) + sh_ref[...]
    u3_ref[...] = u.reshape(u3_ref.shape)
    _router(u, wr_ref[...], br_ref[...], e_ref, gt_ref, rk_ref, cnt_ref, carry)


def ln_router(x, y, counts0, ms, g_chunk, sc_chunk, sh_chunk, ln_g, ln_b, w_r, b_r, alpha):
    n, d = x.shape
    tm = ms.tm
    n_e = w_r.shape[1]
    row = pl.BlockSpec((tm, d), lambda i: (i, 0))
    vec = pl.BlockSpec((1, d), lambda i: (0, 0))
    kk = pl.BlockSpec((tm, MOE_TOPK), lambda i: (i, 0))
    kk_i = jax.ShapeDtypeStruct((n, MOE_TOPK), jnp.int32)
    return pl.pallas_call(
        functools.partial(_ln_router_kernel, alpha=alpha),
        out_shape=(jax.ShapeDtypeStruct((n, d), F32), jax.ShapeDtypeStruct((n, d // LANES, LANES), F32),
                   kk_i, jax.ShapeDtypeStruct((n, MOE_TOPK), F32), kk_i,
                   jax.ShapeDtypeStruct((1, n_e), jnp.int32)),
        grid=(n // tm,),
        in_specs=[row, row, pl.BlockSpec((1, n_e), lambda i: (0, 0)), ms.spec(g_chunk), vec, vec,
                  ms.spec(sc_chunk), ms.spec(sh_chunk),
                  pl.BlockSpec((d, n_e), lambda i: (0, 0)), pl.BlockSpec((1, n_e), lambda i: (0, 0))],
        out_specs=(row, pl.BlockSpec((tm, d // LANES, LANES), lambda i: (i, 0, 0)), kk, kk, kk,
                   pl.BlockSpec((1, n_e), lambda i: (0, 0))),
        scratch_shapes=[pltpu.VMEM((1, n_e), F32)],
        compiler_params=_params("arbitrary"),
    )(x, y, counts0, ms.array, ln_g.reshape(1, d), ln_b.reshape(1, d), ms.array, ms.array, w_r,
      b_r.reshape(1, n_e))


def _ln_combine_kernel(x_ref, gt_ref, ytok_ref, g_ref, lng_ref, lnb_ref, *rest, alpha, do_mod, tm):
    if do_mod:
        sc_ref, sh_ref, xo_ref, uo_ref = rest
    else:
        (xo_ref,) = rest
    gt = gt_ref[...]
    d = x_ref.shape[1]
    yt = ytok_ref[...].reshape((tm, MOE_TOPK) + ytok_ref.shape[1:])
    y = gt[:, 0:1] * yt[:, 0].reshape(tm, d)
    for k in range(1, MOE_TOPK):
        y = y + gt[:, k:k + 1] * yt[:, k].reshape(tm, d)
    xn = _layer_norm(alpha * x_ref[...] + g_ref[...] * y, lng_ref[...], lnb_ref[...])
    xo_ref[...] = xn
    if do_mod:
        uo_ref[...] = (xn * (1.0 + sc_ref[...]) + sh_ref[...]).astype(uo_ref.dtype)


def ln_combine(x, gates, ytok, first_assign, ms, g_chunk, ln_g, ln_b, alpha, next_ms=None, sc_chunk=None,
               sh_chunk=None):
    n, d = x.shape
    tm = ms.tm
    do_mod = next_ms is not None
    assert first_assign % (tm * MOE_TOPK) == 0
    blk0 = first_assign // (tm * MOE_TOPK)
    row = pl.BlockSpec((tm, d), lambda i: (i, 0))
    vec = pl.BlockSpec((1, d), lambda i: (0, 0))
    in_specs = [row, pl.BlockSpec((tm, MOE_TOPK), lambda i: (i, 0)),
                pl.BlockSpec((tm * MOE_TOPK,) + ytok.shape[1:], lambda i: (blk0 + i, 0, 0)),
                ms.spec(g_chunk), vec, vec]
    args = [x, gates, ytok, ms.array, ln_g.reshape(1, d), ln_b.reshape(1, d)]
    out_shape = [jax.ShapeDtypeStruct((n, d), F32)]
    out_specs = [row]
    if do_mod:
        in_specs += [next_ms.spec(sc_chunk), next_ms.spec(sh_chunk)]
        args += [next_ms.array, next_ms.array]
        out_shape.append(jax.ShapeDtypeStruct((n, d), BF16))
        out_specs.append(row)
    res = pl.pallas_call(
        functools.partial(_ln_combine_kernel, alpha=alpha, do_mod=do_mod, tm=tm),
        out_shape=tuple(out_shape),
        grid=(n // tm,), in_specs=in_specs, out_specs=tuple(out_specs),
        compiler_params=_params("arbitrary"),
    )(*args)
    return res if do_mod else (res[0], None)


def _dispatch_kernel(pos_ref, pad_ref, u3_ref, *rest, tm, n_e, tile, n_blk, n_assign, first):
    if first:
        xs_ref, inv_ref, zeros, sem, zsem = rest
    else:
        _, xs_ref, zeros, sem, zsem = rest
    i = pl.program_id(0)
    base = i * (tm * MOE_TOPK)

    def issue(t, carry):
        for k in range(MOE_TOPK):
            a = base + t * MOE_TOPK + k
            row = pos_ref[a]
            pltpu.make_async_copy(u3_ref.at[t], xs_ref.at[row], sem).start()
            if first:
                inv_ref[row] = a
        return carry

    lax.fori_loop(0, tm, issue, 0, unroll=DMA_UNROLL)

    def zero_fill():
        zeros[...] = jnp.zeros_like(zeros)

        def fill_row(r, carry):
            pltpu.make_async_copy(zeros.at[0], xs_ref.at[r], zsem).start()
            return carry

        def fill_later_row(r, carry):
            inv_ref[r] = 0
            return fill_row(r, carry)

        def fill_pad_row(r, spare):
            inv_ref[r] = spare
            return fill_row(r, spare + 1)

        def drain_row(r, carry):
            pltpu.make_async_copy(zeros.at[0], xs_ref.at[0], zsem).wait()
            return carry

        def fill_blk(b, spare):
            pltpu.make_async_copy(zeros, xs_ref.at[pl.ds(b * tile, tile)], zsem).start()

            def mark(r, s):
                inv_ref[b * tile + r] = s
                return s + 1

            return lax.fori_loop(0, tile, mark, spare)

        def drain_blk(b, carry):
            pltpu.make_async_copy(zeros, xs_ref.at[pl.ds(0, tile)], zsem).wait()
            return carry

        spare = n_assign
        for e in range(n_e):
            lax.fori_loop(pad_ref[e], pad_ref[n_e + e], fill_later_row, 0)
            spare = lax.fori_loop(pad_ref[n_e + e], pad_ref[2 * n_e + e], fill_pad_row, spare)
        lax.fori_loop(pad_ref[3 * n_e], n_blk, fill_blk, spare)
        for e in range(n_e):
            lax.fori_loop(pad_ref[e], pad_ref[2 * n_e + e], drain_row, 0)
        lax.fori_loop(pad_ref[3 * n_e], n_blk, drain_blk, 0)

    if first:
        pl.when(i == 0)(zero_fill)
    for k in range(MOE_TOPK):
        pltpu.make_async_copy(u3_ref, xs_ref.at[pl.ds(0, tm)], sem).wait()


def dispatch(u3, pos, pad_bounds, xs, rows, tm, n_e, tile, n_assign):
    n = u3.shape[0]
    tile_shape = u3.shape[1:]
    first = xs is None
    in_specs = [pl.BlockSpec((tm,) + tile_shape, lambda i, p, q: (i, 0, 0))]
    args = [pos, pad_bounds, u3]
    xs_shape = jax.ShapeDtypeStruct((rows,) + tile_shape, F32)
    any_spec = pl.BlockSpec(memory_space=pl.ANY)
    if first:
        out_shape = (xs_shape, jax.ShapeDtypeStruct((rows,), jnp.int32))
        out_specs = (any_spec, pl.BlockSpec(memory_space=pltpu.SMEM))
    else:
        in_specs.append(any_spec)
        args.append(xs)
        out_shape, out_specs = xs_shape, any_spec
    return pl.pallas_call(
        functools.partial(_dispatch_kernel, tm=tm, n_e=n_e, tile=tile, n_blk=rows // tile, n_assign=n_assign,
                          first=first),
        out_shape=out_shape,
        grid_spec=pltpu.PrefetchScalarGridSpec(
            num_scalar_prefetch=2, grid=(n // tm,),
            in_specs=in_specs,
            out_specs=out_specs,
            scratch_shapes=[pltpu.VMEM((tile,) + tile_shape, F32), pltpu.SemaphoreType.DMA,
                            pltpu.SemaphoreType.DMA]),
        input_output_aliases={} if first else {3: 0},
        compiler_params=_params("arbitrary"),
    )(*args)


def _experts_kernel(blk_e_ref, inv_ref, x_ref, w1_ref, b1_ref, w2_ref, b2_ref, ytok_ref, w1bf, w2bf, obuf, sem,
                    *, d_ff, tile, n_blk):
    i = pl.program_id(0)
    blk = jnp.minimum(i, n_blk - 1)
    e = blk_e_ref[blk]
    prev = blk_e_ref[jnp.maximum(blk - 1, 0)]

    @pl.when((i == 0) | (e != prev))
    def _():
        w1bf[...] = w1_ref[...].astype(BF16)
        w2bf[...] = w2_ref[...].astype(BF16)

    def compute():
        x = x_ref[...].reshape(tile, w1bf.shape[0])
        h = jnp.dot(x.astype(BF16), w1bf[...], preferred_element_type=F32) + b1_ref[...]
        gt = jnp.minimum(h[:, :d_ff], SWIGLU_LIMIT)
        up = jnp.clip(h[:, d_ff:], -SWIGLU_LIMIT, SWIGLU_LIMIT)
        act = (up + 1.0) * (gt * jax.nn.sigmoid(SWIGLU_ALPHA * gt))
        y = jnp.dot(act.astype(BF16), w2bf[...], preferred_element_type=F32) + b2_ref[...]
        obuf[i % 2] = y.reshape(obuf.shape[1:])

    def start_scatter():
        half = (i + 1) % 2
        for r in range(tile):
            pltpu.make_async_copy(obuf.at[half, r], ytok_ref.at[inv_ref[(i - 1) * tile + r]], sem).start()

    def wait_scatter():
        pltpu.make_async_copy(obuf.at[0], ytok_ref.at[pl.ds(0, tile)], sem).wait()

    @pl.when(i == 0)
    def _():
        compute()

    @pl.when((i > 0) & (i < n_blk))
    def _():
        start_scatter()
        compute()
        wait_scatter()

    @pl.when(i == n_blk)
    def _():
        start_scatter()
        wait_scatter()


def experts(xs, blk_e, inv, w1, b1, w2, b2, layer, tile):
    rows = xs.shape[0]
    tile_shape = xs.shape[1:]
    n_l, n_e, d, ff2 = w1.shape
    d_ff = ff2 // 2
    n_blk = rows // tile
    cur = lambda i, m: jnp.minimum(i, n_blk - 1)
    per_expert = lambda r, c: pl.BlockSpec((None, None, r, c), lambda i, m, v: (layer, m[cur(i, m)], 0, 0))
    return pl.pallas_call(
        functools.partial(_experts_kernel, d_ff=d_ff, tile=tile, n_blk=n_blk),
        out_shape=jax.ShapeDtypeStruct(xs.shape, F32),
        grid_spec=pltpu.PrefetchScalarGridSpec(
            num_scalar_prefetch=2, grid=(n_blk + 1,),
            in_specs=[pl.BlockSpec((tile,) + tile_shape, lambda i, m, v: (cur(i, m), 0, 0)),
                      per_expert(d, ff2), per_expert(1, ff2), per_expert(d_ff, d), per_expert(1, d)],
            out_specs=pl.BlockSpec(memory_space=pl.ANY),
            scratch_shapes=[pltpu.VMEM((d, ff2), BF16), pltpu.VMEM((d_ff, d), BF16),
                            pltpu.VMEM((2, tile) + tile_shape, F32), pltpu.SemaphoreType.DMA]),
        compiler_params=_params("arbitrary"),
    )(blk_e, inv, xs, w1, b1.reshape(n_l, n_e, 1, ff2), w2, b2.reshape(n_l, n_e, 1, d))


def route_tables(counts, counts_first, tile, n_blk):
    n_e = counts.shape[0]
    padded = (counts + tile - 1) // tile * tile
    p_end = jnp.cumsum(padded)
    start = p_end - padded
    blk_start = jnp.arange(n_blk, dtype=jnp.int32) * tile
    blk_e = jnp.minimum(jnp.sum((p_end[None, :] <= blk_start[:, None]).astype(jnp.int32), axis=1), n_e - 1)
    pad_bounds = jnp.concatenate([start + counts_first, start + counts, p_end, p_end[-1:] // tile])
    return start.astype(jnp.int32), blk_e.astype(jnp.int32), pad_bounds.astype(jnp.int32)


def _block_rank_select(gate, valid, block_id, n_blocks, axis):
    gate = jnp.where(valid, gate, NEG_INF)
    cnt = jnp.zeros(gate.shape, F32)
    for n2 in range(n_blocks):
        g2 = gate[n2:n2 + 1, :] if axis == 0 else gate[:, n2:n2 + 1]
        ahead = (g2 > gate) | ((g2 == gate) & (block_id > n2))
        cnt = cnt + jnp.where(ahead, 1.0, 0.0)
    return valid & (cnt < float(MOBA_TOPK))


MOBA_HEADS_PER_STEP = 2


def _moba_prompt_kernel(q_ref, k_ref, v_ref, o_ref, km_ref, kb_ref, vt_ref, sel_ref, acc_ref, s_ref,
                        *, n_blocks, blk, hd, scale):
    i = pl.program_id(2)
    heads = range(MOBA_HEADS_PER_STEP)
    cols = [slice(g * hd, (g + 1) * hd) for g in heads]

    @pl.when(i == 0)
    def _():
        for g in heads:
            for n in range(n_blocks):
                kblk = k_ref[n * blk:(n + 1) * blk, cols[g]]
                km_ref[g, n:n + 1, :] = jnp.mean(kblk, axis=0, keepdims=True)
                kb_ref[g, n * blk:(n + 1) * blk, :] = kblk.astype(BF16)
                vt_ref[g, n] = v_ref[n * blk:(n + 1) * blk, cols[g]].T.astype(BF16)

    start = pl.multiple_of(i * blk, blk)
    key_id = lax.broadcasted_iota(jnp.int32, (blk, LANES), 0)
    q_id = lax.broadcasted_iota(jnp.int32, (blk, LANES), 1)
    blk_id = lax.broadcasted_iota(jnp.int32, (n_blocks, blk), 0)
    chains = [(g, hf) for g in heads for hf in range(blk // LANES)]
    qtb, state = [], []
    for c, (g, hf) in enumerate(chains):
        qcols = slice(hf * LANES, (hf + 1) * LANES)
        if hf == 0:
            qt = q_ref[:, cols[g]].T
            gate = jnp.dot(km_ref[g], qt, precision=HIGHEST, preferred_element_type=F32)
            sel = jnp.where(_block_rank_select(gate, blk_id < i, blk_id, n_blocks, axis=0), 1.0, 0.0)
            qs = (qt * scale).astype(BF16)
        sel_ref[c] = sel[:, qcols]
        qtb.append(qs[:, qcols])
        s = jnp.dot(kb_ref[g, pl.ds(start, blk), :], qtb[c], preferred_element_type=F32)
        s = jnp.where(key_id <= q_id + hf * LANES, s, NEG_INF)
        m = jnp.max(s, axis=0, keepdims=True)
        p = jnp.exp(s - m)
        acc_ref[c] = jnp.dot(vt_ref[g, i], p.astype(BF16), preferred_element_type=F32)
        state += [m, jnp.sum(p, axis=0, keepdims=True)]
        s_ref[c] = jnp.dot(kb_ref[g, 0:blk, :], qtb[c], preferred_element_type=F32)

    def body(n, carry):
        nxt = pl.multiple_of((n + 1) * blk, blk)
        s_cur = [s_ref[c] for c in range(len(chains))]
        s_nxt = [jnp.dot(kb_ref[g, pl.ds(nxt, blk), :], qtb[c], preferred_element_type=F32)
                 for c, (g, hf) in enumerate(chains)]
        out = []
        for c, (g, hf) in enumerate(chains):
            m, l = carry[2 * c:2 * c + 2]
            s = jnp.where(sel_ref[c, pl.ds(n, 1), :] > 0.0, s_cur[c], NEG_INF)
            m_new = jnp.maximum(m, jnp.max(s, axis=0, keepdims=True))
            a = jnp.exp(m - m_new)
            p = jnp.exp(s - m_new)
            acc_ref[c] = a * acc_ref[c] + jnp.dot(vt_ref[g, n], p.astype(BF16), preferred_element_type=F32)
            out += [m_new, a * l + jnp.sum(p, axis=0, keepdims=True)]
        for c in range(len(chains)):
            s_ref[c] = s_nxt[c]
        return tuple(out)

    state = lax.fori_loop(0, i, body, tuple(state))
    for c, (g, hf) in enumerate(chains):
        o_ref[hf * LANES:(hf + 1) * LANES, cols[g]] = (acc_ref[c] / state[2 * c + 1]).T.astype(o_ref.dtype)


def moba_prompt(qkv, n_seq, seq, heads, hd):
    blk = MOBA_BLOCK
    hpg = MOBA_HEADS_PER_STEP
    assert seq % blk == 0 and heads % hpg == 0
    nq = seq // blk
    hg = heads // hpg
    return pl.pallas_call(
        functools.partial(_moba_prompt_kernel, n_blocks=nq, blk=blk, hd=hd, scale=hd ** -0.5),
        out_shape=jax.ShapeDtypeStruct((n_seq * seq, heads * hd), BF16),
        grid=(n_seq, hg, nq),
        in_specs=[pl.BlockSpec((blk, hpg * hd), lambda b, h, i: (b * nq + i, h)),
                  pl.BlockSpec((seq, hpg * hd), lambda b, h, i: (b, hg + h)),
                  pl.BlockSpec((seq, hpg * hd), lambda b, h, i: (b, 2 * hg + h))],
        out_specs=pl.BlockSpec((blk, hpg * hd), lambda b, h, i: (b * nq + i, h)),
        scratch_shapes=[pltpu.VMEM((hpg, nq, hd), F32), pltpu.VMEM((hpg, seq, hd), BF16),
                        pltpu.VMEM((hpg, nq, hd, blk), BF16), pltpu.VMEM((hpg * (blk // LANES), nq, LANES), F32),
                        pltpu.VMEM((hpg * (blk // LANES), hd, LANES), F32),
                        pltpu.VMEM((hpg * (blk // LANES), blk, LANES), F32)],
        compiler_params=_params("arbitrary", "arbitrary", "arbitrary"),
    )(qkv, qkv, qkv)


PAGES_PER_STEP = 8


def _moba_sample_kernel(pt_ref, q_ref, kn_ref, vn_ref, *rest, n_pages, page, heads, n_past_blocks, scale):
    k_refs = rest[:PAGES_PER_STEP]
    v_refs = rest[PAGES_PER_STEP:2 * PAGES_PER_STEP]
    o_ref, sc_ref, gate_ref, acc_ref, l_ref = rest[2 * PAGES_PER_STEP:]
    ph = pl.program_id(1)
    s = pl.program_id(2)
    pages_per_block = MOBA_BLOCK // page
    q = q_ref[...]
    R, hd = q.shape
    cols = page * heads
    q16 = (q * scale).astype(BF16)
    lane = lax.broadcasted_iota(jnp.int32, (R, LANES), 1)

    def head_of(shape, dim):
        return lax.rem(lax.broadcasted_iota(jnp.int32, shape, dim), heads)

    @pl.when((ph == 0) & (s == 0))
    def _():
        gate_ref[...] = jnp.zeros_like(gate_ref)

    @pl.when(ph == 0)
    def _():
        for r in range(PAGES_PER_STEP):
            k3 = k_refs[r][...]
            sc_ref[s * PAGES_PER_STEP + r] = _nt_dot(q16, k3.reshape(cols, hd).astype(BF16))
            ksum = jnp.sum(k3, axis=0) * (1.0 / MOBA_BLOCK)
            g = jnp.sum(q * jnp.concatenate([ksum] * (R // heads), axis=0), axis=-1, keepdims=True)
            blk_id = (s * PAGES_PER_STEP + r) // pages_per_block
            gate_ref[...] += jnp.where(lane == blk_id, g, 0.0)

    @pl.when((ph == 1) & (s == 0))
    def _():
        sel = _block_rank_select(gate_ref[...], lane < n_past_blocks, lane, n_past_blocks, axis=1)
        same_head = head_of((R, cols), 1) == head_of((R, cols), 0)
        s_own = _nt_dot(q16, kn_ref[...].astype(BF16))
        r_i = lax.broadcasted_iota(jnp.int32, (R, R), 0)
        c_i = lax.broadcasted_iota(jnp.int32, (R, R), 1)
        r_h, c_h = head_of((R, R), 0), head_of((R, R), 1)
        s_own = jnp.where((r_h == c_h) & (c_i - c_h <= r_i - r_h), s_own, NEG_INF)
        mx = jnp.full((R, cols), NEG_INF, F32)
        for pg in range(n_pages):
            b_id = pg // pages_per_block
            mx = jnp.maximum(mx, jnp.where(sel[:, b_id:b_id + 1] & same_head, sc_ref[pg], NEG_INF))
        m = jnp.maximum(jnp.max(s_own, axis=-1, keepdims=True), jnp.max(mx, axis=-1, keepdims=True))
        p_own = jnp.exp(s_own - m)
        psum = jnp.zeros((R, cols), F32)
        for pg in range(n_pages):
            b_id = pg // pages_per_block
            p = jnp.where(sel[:, b_id:b_id + 1] & same_head, jnp.exp(sc_ref[pg] - m), 0.0)
            sc_ref[pg] = p
            psum = psum + p
        l = jnp.sum(p_own, axis=-1, keepdims=True) + jnp.sum(psum, axis=-1, keepdims=True)
        l_ref[...] = jnp.broadcast_to(l, l_ref.shape)
        acc_ref[...] = jnp.dot(p_own.astype(BF16), vn_ref[...].astype(BF16), preferred_element_type=F32)

    @pl.when(ph == 1)
    def _():
        acc = acc_ref[...]
        for r in range(PAGES_PER_STEP):
            p = sc_ref[s * PAGES_PER_STEP + r]
            acc = acc + jnp.dot(p.astype(BF16), v_refs[r][...].reshape(cols, hd).astype(BF16),
                                preferred_element_type=F32)
        acc_ref[...] = acc

    @pl.when((ph == 1) & (s == pl.num_programs(2) - 1))
    def _():
        o_ref[...] = acc_ref[...] / l_ref[:, 0:1]


def moba_sample(qkv, cache_k, cache_v, page_table, layer_j, n_seq, t_new, heads, hd):
    d = heads * hd
    n_pool, n_a, page = cache_k.shape[0], cache_k.shape[1], cache_k.shape[2]
    n_pages = page_table.shape[1]
    past = n_pages * page
    assert past % MOBA_BLOCK == 0 and MOBA_BLOCK % page == 0 and t_new <= MOBA_BLOCK
    assert n_pages % PAGES_PER_STEP == 0
    n_steps = n_pages // PAGES_PER_STEP
    n_past_blocks = past // MOBA_BLOCK
    assert MOBA_TOPK <= n_past_blocks <= 128
    R = heads * t_new
    qkv3 = qkv.reshape(n_seq, t_new, 3 * d)
    q, k_new, v_new = (qkv3[:, :, c * d:(c + 1) * d].reshape(n_seq, R, hd) for c in range(3))
    pt = page_table.reshape(-1).astype(jnp.int32)

    def k_map(r):
        return lambda b, ph, s, pt: (pt[b * n_pages + jnp.where(ph == 0, s, n_steps - 1) * PAGES_PER_STEP + r],
                                     layer_j, 0, 0, 0)

    def v_map(r):
        return lambda b, ph, s, pt: (pt[b * n_pages + jnp.where(ph == 0, 0, s) * PAGES_PER_STEP + r],
                                     layer_j, 0, 0, 0)

    seq_blk = pl.BlockSpec((None, R, hd), lambda b, ph, s, pt: (b, 0, 0))
    page_blk = lambda index_map: pl.BlockSpec((None, None, page, heads, hd), index_map)
    o = pl.pallas_call(
        functools.partial(_moba_sample_kernel, n_pages=n_pages, page=page, heads=heads,
                          n_past_blocks=n_past_blocks, scale=hd ** -0.5),
        out_shape=jax.ShapeDtypeStruct((n_seq, R, hd), F32),
        grid_spec=pltpu.PrefetchScalarGridSpec(
            num_scalar_prefetch=1, grid=(n_seq, 2, n_steps),
            in_specs=[seq_blk, seq_blk, seq_blk]
                     + [page_blk(k_map(r)) for r in range(PAGES_PER_STEP)]
                     + [page_blk(v_map(r)) for r in range(PAGES_PER_STEP)],
            out_specs=seq_blk,
            scratch_shapes=[pltpu.VMEM((n_pages, R, page * heads), F32), pltpu.VMEM((R, LANES), F32),
                            pltpu.VMEM((R, hd), F32), pltpu.VMEM((R, LANES), F32)]),
        compiler_params=_params("arbitrary", "arbitrary", "arbitrary"),
    )(pt, q, k_new, v_new, *([cache_k] * PAGES_PER_STEP), *([cache_v] * PAGES_PER_STEP))
    return o.reshape(n_seq * t_new, d)


def _softplus(x):
    return jnp.maximum(x, 0.0) + jnp.log1p(jnp.exp(-jnp.abs(x)))


def _lru_gates(xc, wra_ref, bra, wix_ref, bix, sp, n, bw):
    cs = slice(n * bw, (n + 1) * bw)
    xcb = xc[:, cs]
    xcb16 = xcb.astype(BF16)
    r = jax.nn.sigmoid(jnp.dot(xcb16, wra_ref[n].astype(BF16), preferred_element_type=F32) + bra[:, cs])
    ig = jax.nn.sigmoid(jnp.dot(xcb16, wix_ref[n].astype(BF16), preferred_element_type=F32) + bix[:, cs])
    log_a = -LRU_C * r * sp[:, cs]
    a = jnp.exp(log_a)
    b = jnp.sqrt(jnp.tanh(-log_a) * (1.0 + a * a)) * (ig * xcb)
    return a, b


def _lru_prompt_kernel(yb_ref, xr_ref, cw_ref, cb_ref, wra_ref, bra_ref, wix_ref, bix_ref, lam_ref,
                       z_ref, cs_ref, hl_ref, xbuf, hcar, *, tc, n_lru_blocks, bw, conv_w):
    i = pl.program_id(1)

    @pl.when(i == 0)
    def _():
        xbuf[0:8, :] = jnp.zeros((8, xbuf.shape[1]), F32)
        hcar[...] = jnp.zeros_like(hcar)

    xr = xr_ref[...]
    xbuf[8:8 + tc, :] = xr
    cw = cw_ref[...]
    xc = cb_ref[...] + xr * cw[conv_w - 1:conv_w, :]
    for s in range(1, conv_w):
        xc = xc + xbuf[8 - s:8 - s + tc, :] * cw[conv_w - 1 - s:conv_w - s, :]
    xbuf[0:8, :] = xbuf[tc:tc + 8, :]
    sp = _softplus(-lam_ref[...])
    bra, bix = bra_ref[...], bix_ref[...]
    row = lax.broadcasted_iota(jnp.int32, (tc, bw), 0)
    for n in range(n_lru_blocks):
        cs = slice(n * bw, (n + 1) * bw)
        a, b = _lru_gates(xc, wra_ref, bra, wix_ref, bix, sp, n, bw)
        sh = 1
        while sh < tc:
            keep = row >= sh
            a_prev = jnp.where(keep, pltpu.roll(a, sh, 0), 1.0)
            b_prev = jnp.where(keep, pltpu.roll(b, sh, 0), 0.0)
            b = a * b_prev + b
            a = a * a_prev
            sh *= 2
        h = a * hcar[:, cs] + b
        hcar[:, cs] = h[tc - 1:tc, :]
        z_ref[:, cs] = (h * jax.nn.gelu(yb_ref[:, cs])).astype(z_ref.dtype)
    cs_ref[...] = xbuf[8 - (conv_w - 1):8, :]
    hl_ref[...] = hcar[...]


def lru_prompt(br, n_seq, seq, conv_w, conv_b, w_ra, b_ra, w_ix, b_ix, lam, tc=256):
    w = br.shape[1] // 2
    cw = conv_w.shape[0]
    nb, bw = w_ra.shape[0], w_ra.shape[1]
    assert seq % tc == 0 and cw - 1 <= 8 <= tc
    nt = seq // tc
    vec = pl.BlockSpec((1, w), lambda b, i: (0, 0))
    wblk = pl.BlockSpec((nb, bw, bw), lambda b, i: (0, 0, 0))
    return pl.pallas_call(
        functools.partial(_lru_prompt_kernel, tc=tc, n_lru_blocks=nb, bw=bw, conv_w=cw),
        out_shape=(jax.ShapeDtypeStruct((n_seq * seq, w), BF16),
                   jax.ShapeDtypeStruct((n_seq, cw - 1, w), F32),
                   jax.ShapeDtypeStruct((n_seq, 1, w), F32)),
        grid=(n_seq, nt),
        in_specs=[pl.BlockSpec((tc, w), lambda b, i: (b * nt + i, 0)),
                  pl.BlockSpec((tc, w), lambda b, i: (b * nt + i, 1)),
                  pl.BlockSpec((cw, w), lambda b, i: (0, 0)), vec, wblk, vec, wblk, vec, vec],
        out_specs=(pl.BlockSpec((tc, w), lambda b, i: (b * nt + i, 0)),
                   pl.BlockSpec((None, cw - 1, w), lambda b, i: (b, 0, 0)),
                   pl.BlockSpec((None, 1, w), lambda b, i: (b, 0, 0))),
        scratch_shapes=[pltpu.VMEM((8 + tc, w), F32), pltpu.VMEM((1, w), F32)],
        compiler_params=_params("arbitrary", "arbitrary"),
    )(br, br, conv_w, conv_b.reshape(1, w), w_ra, b_ra.reshape(1, w), w_ix, b_ix.reshape(1, w), lam.reshape(1, w))


def _lru_sample_kernel(br_ref, c0_ref, h0_ref, cw_ref, cb_ref, wra_ref, bra_ref, wix_ref, bix_ref, lam_ref,
                       z_ref, cs_ref, hl_ref, *, t_new, n_lru_blocks, bw, conv_w, w):
    br = br_ref[...]
    yb, xr = br[:, :w], br[:, w:]
    xpad = jnp.concatenate([c0_ref[...], xr], axis=0)
    cw = cw_ref[...]
    xc = cb_ref[...] + xpad[0:t_new, :] * cw[0:1, :]
    for i in range(1, conv_w):
        xc = xc + xpad[i:i + t_new, :] * cw[i:i + 1, :]
    sp = _softplus(-lam_ref[...])
    bra, bix = bra_ref[...], bix_ref[...]
    gelu_y = jax.nn.gelu(yb)
    for n in range(n_lru_blocks):
        cs = slice(n * bw, (n + 1) * bw)
        a, b = _lru_gates(xc, wra_ref, bra, wix_ref, bix, sp, n, bw)
        h = h0_ref[:, cs]
        for t in range(t_new):
            h = a[t:t + 1, :] * h + b[t:t + 1, :]
            z_ref[t:t + 1, cs] = h * gelu_y[t:t + 1, cs]
        hl_ref[:, cs] = h
    cs_ref[...] = xpad[t_new:t_new + conv_w - 1, :]


def lru_sample(br, conv0, h0, n_seq, t_new, conv_w, conv_b, w_ra, b_ra, w_ix, b_ix, lam):
    w = br.shape[1] // 2
    cw = conv_w.shape[0]
    nb, bw = w_ra.shape[0], w_ra.shape[1]
    vec = pl.BlockSpec((1, w), lambda b: (0, 0))
    wblk = pl.BlockSpec((nb, bw, bw), lambda b: (0, 0, 0))
    per_seq = lambda rows, cols: pl.BlockSpec((None, rows, cols), lambda b: (b, 0, 0))
    z, cs, hl = pl.pallas_call(
        functools.partial(_lru_sample_kernel, t_new=t_new, n_lru_blocks=nb, bw=bw, conv_w=cw, w=w),
        out_shape=(jax.ShapeDtypeStruct((n_seq, t_new, w), F32),
                   jax.ShapeDtypeStruct((n_seq, cw - 1, w), F32),
                   jax.ShapeDtypeStruct((n_seq, 1, w), F32)),
        grid=(n_seq,),
        in_specs=[per_seq(t_new, 2 * w), per_seq(cw - 1, w), per_seq(1, w),
                  pl.BlockSpec((cw, w), lambda b: (0, 0)), vec, wblk, vec, wblk, vec, vec],
        out_specs=(per_seq(t_new, w), per_seq(cw - 1, w), per_seq(1, w)),
        compiler_params=_params("arbitrary"),
    )(br.reshape(n_seq, t_new, 2 * w), conv0, h0.reshape(n_seq, 1, w), conv_w, conv_b.reshape(1, w),
      w_ra, b_ra.reshape(1, w), w_ix, b_ix.reshape(1, w), lam.reshape(1, w))
    return z.reshape(n_seq * t_new, w), cs, hl


def _rope(x, cos, sin, half):
    x1, x2 = x[:, :half], x[:, half:]
    return jnp.concatenate([x1 * cos - x2 * sin, x2 * cos + x1 * sin], axis=-1)


def _group_norm_gate(o, g, gn_g, gn_b):
    mu = jnp.mean(o, axis=-1, keepdims=True)
    d = o - mu
    var = jnp.mean(d * d, axis=-1, keepdims=True)
    on = d * lax.rsqrt(var + LN_EPS) * gn_g + gn_b
    return (g * jax.nn.sigmoid(g)) * on


def _ret_prompt_kernel(q_ref, k_ref, v_ref, g_ref, cos_ref, sin_ref, dm_ref, cd_ref, kd_ref, chd_ref,
                       gng_ref, gnb_ref, z_ref, s_ref, S, *, kscale, half):
    c = pl.program_id(2)

    @pl.when(c == 0)
    def _():
        S[...] = jnp.zeros_like(S)

    cos, sin = cos_ref[...], sin_ref[...]
    q = _rope(q_ref[...], cos, sin, half)
    k = _rope(k_ref[...] * kscale, cos, sin, half)
    qb, kb, vb = q.astype(BF16), k.astype(BF16), v_ref[...].astype(BF16)
    inner = _nt_dot(qb, kb) * dm_ref[...]
    s_old = S[...]
    o = (jnp.dot(inner.astype(BF16), vb, preferred_element_type=F32)
         + jnp.dot(qb, s_old.astype(BF16), preferred_element_type=F32) * cd_ref[...])
    kdt = (k * kd_ref[...]).T.astype(BF16)
    S[...] = chd_ref[...] * s_old + jnp.dot(kdt, vb, preferred_element_type=F32)
    z_ref[...] = _group_norm_gate(o, g_ref[...], gng_ref[...], gnb_ref[...]).astype(z_ref.dtype)

    @pl.when(c == pl.num_programs(2) - 1)
    def _():
        s_ref[...] = S[...]


def _decay_tables(heads, c):
    f = np.float32
    lg = np.log1p(-np.exp2(f(-5.0) - np.arange(heads, dtype=f))).astype(f)
    idx = np.arange(c, dtype=f)
    diff = idx[:, None] - idx[None, :]
    dmask = np.where(diff >= 0, np.exp(np.maximum(diff, f(0.0))[None] * lg[:, None, None]), f(0.0)).astype(f)
    cross = np.exp((idx + f(1.0))[None, :] * lg[:, None])[..., None].astype(f)
    kdec = np.exp((f(c) - f(1.0) - idx)[None, :] * lg[:, None])[..., None].astype(f)
    chunk = np.exp(f(c) * lg)[:, None, None].astype(f)
    return dmask, cross, kdec, chunk


def _rope_tables(pos0, t, half):
    f = np.float32
    inv = (f(ROPE_BASE) ** (-np.arange(half, dtype=f) / f(half))).astype(f)
    ang = (np.arange(pos0, pos0 + t).astype(f)[:, None] * inv[None, :]).astype(f)
    return np.cos(ang).astype(f), np.sin(ang).astype(f)


def ret_prompt(proj, n_seq, seq, heads, kdim, vdim, gn_g, gn_b, tc=256):
    assert seq % tc == 0 and vdim % kdim == 0
    nc = seq // tc
    hk, hv = heads * kdim, heads * vdim
    half = kdim // 2
    cos, sin = _rope_tables(0, seq, half)
    dmask, cross, kdec, chunk = _decay_tables(heads, tc)
    v0, g0 = 2 * hk // vdim, (2 * hk + hv) // vdim
    row = lambda b, h, c: b * nc + c
    per_head = lambda r, cdim: pl.BlockSpec((None, r, cdim), lambda b, h, c: (h, 0, 0))
    return pl.pallas_call(
        functools.partial(_ret_prompt_kernel, kscale=kdim ** -0.5, half=half),
        out_shape=(jax.ShapeDtypeStruct((n_seq * seq, hv), BF16),
                   jax.ShapeDtypeStruct((n_seq, heads, kdim, vdim), F32)),
        grid=(n_seq, heads, nc),
        in_specs=[pl.BlockSpec((tc, kdim), lambda b, h, c: (row(b, h, c), h)),
                  pl.BlockSpec((tc, kdim), lambda b, h, c: (row(b, h, c), heads + h)),
                  pl.BlockSpec((tc, vdim), lambda b, h, c: (row(b, h, c), v0 + h)),
                  pl.BlockSpec((tc, vdim), lambda b, h, c: (row(b, h, c), g0 + h)),
                  pl.BlockSpec((tc, half), lambda b, h, c: (c, 0)),
                  pl.BlockSpec((tc, half), lambda b, h, c: (c, 0)),
                  per_head(tc, tc), per_head(tc, 1), per_head(tc, 1), per_head(1, 1),
                  pl.BlockSpec((1, vdim), lambda b, h, c: (0, h)),
                  pl.BlockSpec((1, vdim), lambda b, h, c: (0, h))],
        out_specs=(pl.BlockSpec((tc, vdim), lambda b, h, c: (row(b, h, c), h)),
                   pl.BlockSpec((None, None, kdim, vdim), lambda b, h, c: (b, h, 0, 0))),
        scratch_shapes=[pltpu.VMEM((kdim, vdim), F32)],
        compiler_params=_params("arbitrary", "arbitrary", "arbitrary"),
    )(proj, proj, proj, proj, cos, sin, dmask, cross, kdec, chunk, gn_g.reshape(1, hv), gn_b.reshape(1, hv))


def _ret_sample_kernel(p_ref, kt_ref, s0_ref, cos_ref, sin_ref, cost_ref, sint_ref, dm_ref, cd_ref, kd_ref, chd_ref,
                       gng_ref, gnb_ref, z_ref, s_ref, *, heads, kdim, vdim, t_new, kscale):
    half = kdim // 2
    hk, hv = heads * kdim, heads * vdim
    p = p_ref[...]
    cos, sin = cos_ref[...], sin_ref[...]
    cost, sint = cost_ref[...], sint_ref[...]
    for h in range(heads):
        q = _rope(p[:, h * kdim:(h + 1) * kdim], cos, sin, half)
        k = _rope(p[:, hk + h * kdim:hk + (h + 1) * kdim] * kscale, cos, sin, half)
        v = p[:, 2 * hk + h * vdim:2 * hk + (h + 1) * vdim]
        g = p[:, 2 * hk + hv + h * vdim:2 * hk + hv + (h + 1) * vdim]
        kt = kt_ref[h] * kscale
        k1, k2 = kt[:half, :], kt[half:, :]
        kt = jnp.concatenate([k1 * cost - k2 * sint, k2 * cost + k1 * sint], axis=0) * kd_ref[h]
        s0 = s0_ref[h]
        dm = dm_ref[h]
        o = jnp.dot(q, s0, preferred_element_type=F32, precision=HIGHEST) * cd_ref[h]
        s_new = chd_ref[h] * s0
        for m in range(t_new):
            inner_m = jnp.sum(q * k[m:m + 1, :], axis=-1, keepdims=True) * dm[:, m:m + 1]
            o = o + inner_m * v[m:m + 1, :]
            s_new = s_new + kt[:, m:m + 1] * v[m:m + 1, :]
        s_ref[h] = s_new
        z_ref[:, h * vdim:(h + 1) * vdim] = _group_norm_gate(
            o, g, gng_ref[:, h * vdim:(h + 1) * vdim], gnb_ref[:, h * vdim:(h + 1) * vdim])


def ret_sample(proj, s0, pos0, n_seq, t_new, heads, kdim, vdim, gn_g, gn_b):
    hk, hv = heads * kdim, heads * vdim
    half = kdim // 2
    width = 2 * hk + 2 * hv
    cos, sin = _rope_tables(pos0, t_new, half)
    dmask, cross, kdec, chunk = _decay_tables(heads, t_new)
    proj3 = proj.reshape(n_seq, t_new, width)
    kt = proj3[:, :, hk:2 * hk].reshape(n_seq, t_new, heads, kdim).transpose(0, 2, 3, 1)
    full = lambda *shape: pl.BlockSpec(shape, lambda b: (0,) * len(shape))
    z, s = pl.pallas_call(
        functools.partial(_ret_sample_kernel, heads=heads, kdim=kdim, vdim=vdim, t_new=t_new, kscale=kdim ** -0.5),
        out_shape=(jax.ShapeDtypeStruct((n_seq, t_new, hv), F32),
                   jax.ShapeDtypeStruct((n_seq, heads, kdim, vdim), F32)),
        grid=(n_seq,),
        in_specs=[pl.BlockSpec((None, t_new, width), lambda b: (b, 0, 0)),
                  pl.BlockSpec((None, heads, kdim, t_new), lambda b: (b, 0, 0, 0)),
                  pl.BlockSpec((None, heads, kdim, vdim), lambda b: (b, 0, 0, 0)),
                  full(t_new, half), full(t_new, half), full(half, t_new), full(half, t_new),
                  full(heads, t_new, t_new), full(heads, t_new, 1), full(heads, 1, t_new), full(heads, 1, 1),
                  full(1, hv), full(1, hv)],
        out_specs=(pl.BlockSpec((None, t_new, hv), lambda b: (b, 0, 0)),
                   pl.BlockSpec((None, heads, kdim, vdim), lambda b: (b, 0, 0, 0))),
        compiler_params=_params("arbitrary"),
    )(proj3, kt, s0, cos, sin, cos.T, sin.T, dmask, cross, kdec.transpose(0, 2, 1), chunk,
      gn_g.reshape(1, hv), gn_b.reshape(1, hv))
    return z.reshape(n_seq * t_new, hv), s


def kernel(x_prompt, x_sample, cache_k, cache_v, state_lru_conv, state_lru_h, state_ret, page_table, c_prompt, c_sample, w_ada, b_ada, ln_g, ln_b, a_w_qkv, a_w_o, lru_w_in, lru_b_in, lru_conv_w, lru_conv_b, lru_w_ra, lru_b_ra, lru_w_ix, lru_b_ix, lru_lambda, lru_w_out, ret_w_in, ret_gn_g, ret_gn_b, ret_w_o, moe_w_router, moe_b_router, moe_w1, moe_b1, moe_w2, moe_b2):
    depth, d = w_ada.shape[0], w_ada.shape[1]
    n_mixers = 3
    alpha = (2 * depth) ** 0.25
    heads, hd = cache_k.shape[3], cache_k.shape[4]
    r_heads, r_kdim, r_vdim = state_ret.shape[2], state_ret.shape[3], state_ret.shape[4]
    n_e = moe_w_router.shape[2]
    past_len = page_table.shape[1] * cache_k.shape[2]

    bp, bs = c_prompt.shape[0], c_sample.shape[0]
    n_c = bp + bs
    c_rows = 16 * (-(-n_c // 16))
    c_all = jnp.pad(jnp.concatenate([c_prompt, c_sample], axis=0), ((0, c_rows - n_c), (0, 0)))
    mods = [matmul(c_all, w_ada, b_ada, layer=l, pre_silu=True) for l in range(depth)]

    class Group:
        def __init__(self, x, seq_lo, sample):
            self.sample = sample
            self.n_seq, self.t, _ = x.shape
            self.n = self.n_seq * self.t
            self.tm = min(ROW_TILE, self.n)
            self.ms = [ModSource(mods[l][seq_lo:seq_lo + self.n_seq], self.t, self.tm, d) for l in range(depth)]
            self.x2 = x.reshape(self.n, d)
            self.u = modulate(self.x2, self.ms[0], 1, 0)
            self.ks, self.vs, self.convs, self.hs, self.ss = [], [], [], [], []
            self.k_all = self.v_all = None

        def outputs(self):
            if self.k_all is None:
                lead = (self.n_seq, self.t, heads, hd)
                self.k_all = jnp.stack([k.reshape(lead) for k in self.ks], 1)
                self.v_all = jnp.stack([v.reshape(lead) for v in self.vs], 1)
            return (self.x2.reshape(self.n_seq, self.t, d), self.k_all, self.v_all,
                    jnp.stack(self.convs, 1), jnp.stack(self.hs, 1), jnp.stack(self.ss, 1))

    def mixer(g, layer):
        kind, j = layer % n_mixers, layer // n_mixers
        u, sample, n_seq, t = g.u, g.sample, g.n_seq, g.t
        if kind == 0:
            if sample:
                qkv = matmul(u, a_w_qkv, layer=j)
                o = moba_sample(qkv, cache_k, cache_v, page_table, j, n_seq, t, heads, hd)
                k3, v3 = qkv[:, d:2 * d], qkv[:, 2 * d:]
                g.ks.append(k3.reshape(n_seq, t, heads, hd))
                g.vs.append(v3.reshape(n_seq, t, heads, hd))
            elif 0 < j == a_w_qkv.shape[0] - 1:
                qkv, g.k_all, g.v_all = qkv_project(u, a_w_qkv[j], heads, hd, n_seq, t, g.ks, g.vs)
                o = moba_prompt(qkv, n_seq, t, heads, hd)
            else:
                qkv, k3, v3 = qkv_project(u, a_w_qkv[j], heads, hd, n_seq, t)
                o = moba_prompt(qkv, n_seq, t, heads, hd)
                g.ks.append(k3)
                g.vs.append(v3)
            return matmul(o, a_w_o, layer=j)
        if kind == 1:
            br = matmul(u, lru_w_in[j], lru_b_in[j])
            lru_args = (lru_conv_w[j], lru_conv_b[j], lru_w_ra[j], lru_b_ra[j], lru_w_ix[j], lru_b_ix[j],
                        lru_lambda[j])
            if sample:
                z, cs, hl = lru_sample(br, state_lru_conv[:, j], state_lru_h[:, j], n_seq, t, *lru_args)
            else:
                z, cs, hl = lru_prompt(br, n_seq, t, *lru_args)
            g.convs.append(cs)
            g.hs.append(hl.reshape(n_seq, -1))
            return matmul(z, lru_w_out[j])
        proj = matmul(u, ret_w_in[j])
        if sample:
            z, s_fin = ret_sample(proj, state_ret[:, j], past_len, n_seq, t, r_heads, r_kdim, r_vdim,
                                  ret_gn_g[j], ret_gn_b[j])
        else:
            z, s_fin = ret_prompt(proj, n_seq, t, r_heads, r_kdim, r_vdim, ret_gn_g[j], ret_gn_b[j])
        g.ss.append(s_fin)
        return matmul(z, ret_w_o[j])

    groups = [Group(x_prompt, 0, False), Group(x_sample, bp, True)]
    tile = MOE_TILE
    n_assign = sum(g.n for g in groups) * MOE_TOPK
    n_blk = -(-n_assign // tile) + n_e
    for layer in range(depth):
        counts = [jnp.zeros((1, n_e), jnp.int32)]
        routed = []
        for g in groups:
            y = mixer(g, layer)
            g.x2, u3, top_e, gates, rank, cnt = ln_router(
                g.x2, y, counts[-1], g.ms[layer], 2, 4, 3, ln_g[layer, 0], ln_b[layer, 0],
                moe_w_router[layer], moe_b_router[layer], alpha)
            counts.append(cnt)
            routed.append((u3, top_e, gates, rank))
        start, blk_e, pad_bounds = route_tables(counts[-1].reshape(n_e), counts[1].reshape(n_e), tile, n_blk)
        xs, inv, first_assign = None, None, []
        for g, (u3, top_e, gates, rank) in zip(groups, routed):
            e_ids = jnp.arange(n_e, dtype=jnp.int32)
            first_row = jnp.sum(jnp.where(top_e[..., None] == e_ids, start, 0), axis=-1)
            pos = (first_row + rank).reshape(g.n * MOE_TOPK)
            first_assign.append(sum(h.n for h in groups[:len(first_assign)]) * MOE_TOPK)
            if xs is None:
                xs, inv = dispatch(u3, pos, pad_bounds, None, n_blk * tile, g.tm, n_e, tile, n_assign)
            else:
                xs = dispatch(u3, pos, pad_bounds, xs, n_blk * tile, g.tm, n_e, tile, n_assign)
                inv = inv.at[pos].set(first_assign[-1] + jnp.arange(g.n * MOE_TOPK, dtype=jnp.int32))
        ytok = experts(xs, blk_e, inv, moe_w1, moe_b1, moe_w2, moe_b2, layer, tile)
        for g, (u3, top_e, gates, rank), a0 in zip(groups, routed, first_assign):
            nxt = (g.ms[layer + 1], 1, 0) if layer + 1 < depth else ()
            g.x2, g.u = ln_combine(g.x2, gates, ytok, a0, g.ms[layer], 5, ln_g[layer, 1], ln_b[layer, 1], alpha,
                                   *nxt)

    y_p, k_p, v_p, conv_p, h_p, s_p = groups[0].outputs()
    y_s, k_s, v_s, conv_s, h_s, s_s = groups[1].outputs()
    return (y_p, y_s, k_p, v_p, conv_p, h_p, s_p, k_s, v_s, conv_s, h_s, s_s)
```

```python
import functools

import numpy as np
import jax
import jax.numpy as jnp
from jax import lax
from jax.experimental import pallas as pl
from jax.experimental.pallas import tpu as pltpu

F32 = jnp.float32
BF16 = jnp.bfloat16
HIGHEST = lax.Precision.HIGHEST
NEG_INF = float("-inf")

MOBA_BLOCK = 256
MOBA_TOPK = 3
MOE_TOPK = 4
LRU_C = 8.0
ROPE_BASE = 10000.0
SWIGLU_LIMIT = 7.0
SWIGLU_ALPHA = 1.702
LN_EPS = 1e-5

LANES = 128
DMA_UNROLL = 8
MOE_TILE = 256
ROW_TILE = 256
V7X_VMEM_BYTES = 64 * 2**20
VMEM_LIMIT = V7X_VMEM_BYTES - 8 * 2**20


def _params(*sem):
    return pltpu.CompilerParams(dimension_semantics=sem, vmem_limit_bytes=VMEM_LIMIT)


def _nt_dot(a, b, **kw):
    return lax.dot_general(a, b, (((1,), (1,)), ((), ())), preferred_element_type=F32, **kw)


def _mm_kernel(*refs, has_bias, pre_silu):
    if has_bias:
        x_ref, w_ref, b_ref, o_ref, wbf_ref = refs
    else:
        x_ref, w_ref, o_ref, wbf_ref = refs

    @pl.when(pl.program_id(1) == 0)
    def _():
        wbf_ref[...] = w_ref[...].astype(BF16)

    x = x_ref[...]
    if pre_silu:
        x = x * jax.nn.sigmoid(x)
    acc = jnp.dot(x.astype(BF16), wbf_ref[...], preferred_element_type=F32)
    if has_bias:
        acc = acc + b_ref[...]
    o_ref[...] = acc


def matmul(x, w, b=None, *, layer=None, pre_silu=False, tm=1024, tn=1024):
    M, K = x.shape
    N = w.shape[-1]
    tm, tn = min(tm, M), min(tn, N)
    assert M % tm == 0 and N % tn == 0, (M, N, tm, tn)
    if layer is None:
        w_spec = pl.BlockSpec((K, tn), lambda j, i: (0, j))
        b_spec = pl.BlockSpec((1, tn), lambda j, i: (0, j))
        b = None if b is None else b.reshape(1, N)
    else:
        w_spec = pl.BlockSpec((None, K, tn), lambda j, i: (layer, 0, j))
        b_spec = pl.BlockSpec((None, 1, tn), lambda j, i: (layer, 0, j))
        b = None if b is None else b.reshape(b.shape[0], 1, N)
    in_specs = [pl.BlockSpec((tm, K), lambda j, i: (i, 0)), w_spec]
    args = [x, w]
    if b is not None:
        in_specs.append(b_spec)
        args.append(b)
    return pl.pallas_call(
        functools.partial(_mm_kernel, has_bias=b is not None, pre_silu=pre_silu),
        out_shape=jax.ShapeDtypeStruct((M, N), F32),
        grid=(N // tn, M // tm),
        in_specs=in_specs,
        out_specs=pl.BlockSpec((tm, tn), lambda j, i: (i, j)),
        scratch_shapes=[pltpu.VMEM((K, tn), BF16)],
        compiler_params=_params("arbitrary", "arbitrary"),
    )(*args)


def _qkv_kernel(x_ref, w_ref, *rest, d, n_prev):
    prev_k, prev_v = rest[:n_prev], rest[n_prev:2 * n_prev]
    qkv_ref, k_ref, v_ref, wbf_ref = rest[2 * n_prev:]

    @pl.when(pl.program_id(0) == 0)
    def _():
        wbf_ref[...] = w_ref[...].astype(BF16)

    acc = jnp.dot(x_ref[...].astype(BF16), wbf_ref[...], preferred_element_type=F32)
    qkv_ref[...] = acc
    leaf = k_ref.shape[-3:]
    k_new, v_new = acc[:, d:2 * d].reshape(leaf), acc[:, 2 * d:].reshape(leaf)
    if n_prev == 0:
        k_ref[...] = k_new
        v_ref[...] = v_new
    else:
        for p in range(n_prev):
            k_ref[p] = prev_k[p][...]
            v_ref[p] = prev_v[p][...]
        k_ref[n_prev] = k_new
        v_ref[n_prev] = v_new


def qkv_project(u, w, heads, hd, n_seq, seq, prev_k=(), prev_v=(), tm=256):
    n, d = u.shape
    assert n == n_seq * seq and seq % tm == 0 and w.shape == (d, 3 * d) and heads * hd == d
    n_prev = len(prev_k)
    tps = seq // tm
    tok_blk = pl.BlockSpec((tm, heads, hd), lambda i: (i, 0, 0))
    if n_prev == 0:
        leaf, leaf_blk = jax.ShapeDtypeStruct((n, heads, hd), F32), tok_blk
    else:
        leaf = jax.ShapeDtypeStruct((n_seq, n_prev + 1, seq, heads, hd), F32)
        leaf_blk = pl.BlockSpec((None, n_prev + 1, tm, heads, hd), lambda i: (i // tps, 0, i % tps, 0, 0))
    return pl.pallas_call(
        functools.partial(_qkv_kernel, d=d, n_prev=n_prev),
        out_shape=(jax.ShapeDtypeStruct((n, 3 * d), F32), leaf, leaf),
        grid=(n // tm,),
        in_specs=[pl.BlockSpec((tm, d), lambda i: (i, 0)),
                  pl.BlockSpec((d, 3 * d), lambda i: (0, 0), pipeline_mode=pl.Buffered(1))]
                 + [tok_blk] * (2 * n_prev),
        out_specs=(pl.BlockSpec((tm, 3 * d), lambda i: (i, 0)), leaf_blk, leaf_blk),
        scratch_shapes=[pltpu.VMEM((d, 3 * d), BF16)],
        compiler_params=_params("arbitrary"),
    )(u, w, *prev_k, *prev_v)


class ModSource:
    def __init__(self, mod, rows_per_seq, tm, d):
        self.d = d
        self.tm = tm
        if rows_per_seq % tm == 0:
            self.per_seq = True
            self.tiles_per_seq = rows_per_seq // tm
            self.array = mod.reshape(mod.shape[0], 1, mod.shape[1])
        else:
            self.per_seq = False
            self.array = jnp.repeat(mod, rows_per_seq, axis=0)

    def spec(self, chunk):
        if self.per_seq:
            tps = self.tiles_per_seq
            return pl.BlockSpec((None, 1, self.d), lambda i, *_: (i // tps, 0, chunk))
        return pl.BlockSpec((self.tm, self.d), lambda i, *_: (i, chunk))


def _layer_norm(z, g, b):
    mu = jnp.mean(z, axis=-1, keepdims=True)
    d = z - mu
    var = jnp.mean(d * d, axis=-1, keepdims=True)
    return d * lax.rsqrt(var + LN_EPS) * g + b


def _router(u, wr, br, e_ref, gt_ref, rk_ref, cnt_ref, carry):
    logits = jnp.dot(u, wr, preferred_element_type=F32, precision=HIGHEST) + br
    tm, n_e = logits.shape
    lane = lax.broadcasted_iota(jnp.int32, logits.shape, 1).astype(F32)
    slot = lax.broadcasted_iota(jnp.int32, (tm, MOE_TOPK), 1)
    idx_out = jnp.zeros((tm, MOE_TOPK), F32)
    val_out = jnp.zeros((tm, MOE_TOPK), F32)
    cur = logits
    top = None
    picks = []
    for k in range(MOE_TOPK):
        m = jnp.max(cur, axis=-1, keepdims=True)
        idx = jnp.min(jnp.where(cur == m, lane, float(n_e)), axis=-1, keepdims=True)
        if top is None:
            top = m
        idx_out = jnp.where(slot == k, idx, idx_out)
        val_out = jnp.where(slot == k, m, val_out)
        picks.append(lane == idx)
        cur = jnp.where(picks[-1], NEG_INF, cur)
    ex = jnp.exp(val_out - top)
    gt_ref[...] = ex / jnp.sum(ex, axis=-1, keepdims=True)
    e_ref[...] = idx_out.astype(jnp.int32)
    hot = sum(jnp.where(pk, 1.0, 0.0) for pk in picks)
    r_id = lax.broadcasted_iota(jnp.int32, (tm, tm), 0)
    c_id = lax.broadcasted_iota(jnp.int32, (tm, tm), 1)
    tri = jnp.where(c_id < r_id, 1.0, 0.0).astype(BF16)
    before = jnp.dot(tri, hot.astype(BF16), preferred_element_type=F32) + carry[...]
    rk_out = jnp.zeros((tm, MOE_TOPK), F32)
    for k in range(MOE_TOPK):
        rk = jnp.sum(jnp.where(picks[k], before, 0.0), axis=-1, keepdims=True)
        rk_out = jnp.where(slot == k, rk, rk_out)
    rk_ref[...] = rk_out.astype(jnp.int32)
    carry[...] += jnp.sum(hot, axis=0, keepdims=True)
    cnt_ref[...] = carry[...].astype(jnp.int32)


def _mod_kernel(x_ref, sc_ref, sh_ref, u_ref):
    u_ref[...] = (x_ref[...] * (1.0 + sc_ref[...]) + sh_ref[...]).astype(u_ref.dtype)


def modulate(x, ms, sc_chunk, sh_chunk):
    n, d = x.shape
    tm = ms.tm
    row = pl.BlockSpec((tm, d), lambda i: (i, 0))
    return pl.pallas_call(
        _mod_kernel,
        out_shape=jax.ShapeDtypeStruct((n, d), BF16),
        grid=(n // tm,),
        in_specs=[row, ms.spec(sc_chunk), ms.spec(sh_chunk)],
        out_specs=row,
        compiler_params=_params("arbitrary"),
    )(x, ms.array, ms.array)


def _ln_router_kernel(x_ref, y_ref, cnt0_ref, g_ref, lng_ref, lnb_ref, sc_ref, sh_ref, wr_ref, br_ref,
                      xo_ref, u3_ref, e_ref, gt_ref, rk_ref, cnt_ref, carry, *, alpha):
    @pl.when(pl.program_id(0) == 0)
    def _():
        carry[...] = cnt0_ref[...].astype(F32)

    xn = _layer_norm(alpha * x_ref[...] + g_ref[...] * y_ref[...], lng_ref[...], lnb_ref[...])
    xo_ref[...] = xn
    u = xn * (1.0 + sc_ref[...]) + sh_ref[...]
    u3_ref[...] = u.reshape(u3_ref.shape)
    _router(u, wr_ref[...], br_ref[...], e_ref, gt_ref, rk_ref, cnt_ref, carry)


def ln_router(x, y, counts0, ms, g_chunk, sc_chunk, sh_chunk, ln_g, ln_b, w_r, b_r, alpha):
    n, d = x.shape
    tm = ms.tm
    n_e = w_r.shape[1]
    row = pl.BlockSpec((tm, d), lambda i: (i, 0))
    vec = pl.BlockSpec((1, d), lambda i: (0, 0))
    kk = pl.BlockSpec((tm, MOE_TOPK), lambda i: (i, 0))
    kk_i = jax.ShapeDtypeStruct((n, MOE_TOPK), jnp.int32)
    return pl.pallas_call(
        functools.partial(_ln_router_kernel, alpha=alpha),
        out_shape=(jax.ShapeDtypeStruct((n, d), F32), jax.ShapeDtypeStruct((n, d // LANES, LANES), F32),
                   kk_i, jax.ShapeDtypeStruct((n, MOE_TOPK), F32), kk_i,
                   jax.ShapeDtypeStruct((1, n_e), jnp.int32)),
        grid=(n // tm,),
        in_specs=[row, row, pl.BlockSpec((1, n_e), lambda i: (0, 0)), ms.spec(g_chunk), vec, vec,
                  ms.spec(sc_chunk), ms.spec(sh_chunk),
                  pl.BlockSpec((d, n_e), lambda i: (0, 0)), pl.BlockSpec((1, n_e), lambda i: (0, 0))],
        out_specs=(row, pl.BlockSpec((tm, d // LANES, LANES), lambda i: (i, 0, 0)), kk, kk, kk,
                   pl.BlockSpec((1, n_e), lambda i: (0, 0))),
        scratch_shapes=[pltpu.VMEM((1, n_e), F32)],
        compiler_params=_params("arbitrary"),
    )(x, y, counts0, ms.array, ln_g.reshape(1, d), ln_b.reshape(1, d), ms.array, ms.array, w_r,
      b_r.reshape(1, n_e))


def _ln_combine_kernel(x_ref, gt_ref, ytok_ref, g_ref, lng_ref, lnb_ref, *rest, alpha, do_mod, tm):
    if do_mod:
        sc_ref, sh_ref, xo_ref, uo_ref = rest
    else:
        (xo_ref,) = rest
    gt = gt_ref[...]
    d = x_ref.shape[1]
    yt = ytok_ref[...].reshape((tm, MOE_TOPK) + ytok_ref.shape[1:])
    y = gt[:, 0:1] * yt[:, 0].reshape(tm, d)
    for k in range(1, MOE_TOPK):
        y = y + gt[:, k:k + 1] * yt[:, k].reshape(tm, d)
    xn = _layer_norm(alpha * x_ref[...] + g_ref[...] * y, lng_ref[...], lnb_ref[...])
    xo_ref[...] = xn
    if do_mod:
        uo_ref[...] = (xn * (1.0 + sc_ref[...]) + sh_ref[...]).astype(uo_ref.dtype)


def ln_combine(x, gates, ytok, first_assign, ms, g_chunk, ln_g, ln_b, alpha, next_ms=None, sc_chunk=None,
               sh_chunk=None):
    n, d = x.shape
    tm = ms.tm
    do_mod = next_ms is not None
    assert first_assign % (tm * MOE_TOPK) == 0
    blk0 = first_assign // (tm * MOE_TOPK)
    row = pl.BlockSpec((tm, d), lambda i: (i, 0))
    vec = pl.BlockSpec((1, d), lambda i: (0, 0))
    in_specs = [row, pl.BlockSpec((tm, MOE_TOPK), lambda i: (i, 0)),
                pl.BlockSpec((tm * MOE_TOPK,) + ytok.shape[1:], lambda i: (blk0 + i, 0, 0)),
                ms.spec(g_chunk), vec, vec]
    args = [x, gates, ytok, ms.array, ln_g.reshape(1, d), ln_b.reshape(1, d)]
    out_shape = [jax.ShapeDtypeStruct((n, d), F32)]
    out_specs = [row]
    if do_mod:
        in_specs += [next_ms.spec(sc_chunk), next_ms.spec(sh_chunk)]
        args += [next_ms.array, next_ms.array]
        out_shape.append(jax.ShapeDtypeStruct((n, d), BF16))
        out_specs.append(row)
    res = pl.pallas_call(
        functools.partial(_ln_combine_kernel, alpha=alpha, do_mod=do_mod, tm=tm),
        out_shape=tuple(out_shape),
        grid=(n // tm,), in_specs=in_specs, out_specs=tuple(out_specs),
        compiler_params=_params("arbitrary"),
    )(*args)
    return res if do_mod else (res[0], None)


def _dispatch_kernel(pos_ref, pad_ref, u3_ref, *rest, tm, n_e, tile, n_blk, n_assign, first):
    if first:
        xs_ref, inv_ref, zeros, sem, zsem = rest
    else:
        _, xs_ref, zeros, sem, zsem = rest
    i = pl.program_id(0)
    base = i * (tm * MOE_TOPK)

    def issue(t, carry):
        for k in range(MOE_TOPK):
            a = base + t * MOE_TOPK + k
            row = pos_ref[a]
            pltpu.make_async_copy(u3_ref.at[t], xs_ref.at[row], sem).start()
            if first:
                inv_ref[row] = a
        return carry

    lax.fori_loop(0, tm, issue, 0, unroll=DMA_UNROLL)

    def zero_fill():
        zeros[...] = jnp.zeros_like(zeros)

        def fill_row(r, carry):
            pltpu.make_async_copy(zeros.at[0], xs_ref.at[r], zsem).start()
            return carry

        def fill_later_row(r, carry):
            inv_ref[r] = 0
            return fill_row(r, carry)

        def fill_pad_row(r, spare):
            inv_ref[r] = spare
            return fill_row(r, spare + 1)

        def drain_row(r, carry):
            pltpu.make_async_copy(zeros.at[0], xs_ref.at[0], zsem).wait()
            return carry

        def fill_blk(b, spare):
            pltpu.make_async_copy(zeros, xs_ref.at[pl.ds(b * tile, tile)], zsem).start()

            def mark(r, s):
                inv_ref[b * tile + r] = s
                return s + 1

            return lax.fori_loop(0, tile, mark, spare)

        def drain_blk(b, carry):
            pltpu.make_async_copy(zeros, xs_ref.at[pl.ds(0, tile)], zsem).wait()
            return carry

        spare = n_assign
        for e in range(n_e):
            lax.fori_loop(pad_ref[e], pad_ref[n_e + e], fill_later_row, 0)
            spare = lax.fori_loop(pad_ref[n_e + e], pad_ref[2 * n_e + e], fill_pad_row, spare)
        lax.fori_loop(pad_ref[3 * n_e], n_blk, fill_blk, spare)
        for e in range(n_e):
            lax.fori_loop(pad_ref[e], pad_ref[2 * n_e + e], drain_row, 0)
        lax.fori_loop(pad_ref[3 * n_e], n_blk, drain_blk, 0)

    if first:
        pl.when(i == 0)(zero_fill)
    for k in range(MOE_TOPK):
        pltpu.make_async_copy(u3_ref, xs_ref.at[pl.ds(0, tm)], sem).wait()


def dispatch(u3, pos, pad_bounds, xs, rows, tm, n_e, tile, n_assign):
    n = u3.shape[0]
    tile_shape = u3.shape[1:]
    first = xs is None
    in_specs = [pl.BlockSpec((tm,) + tile_shape, lambda i, p, q: (i, 0, 0))]
    args = [pos, pad_bounds, u3]
    xs_shape = jax.ShapeDtypeStruct((rows,) + tile_shape, F32)
    any_spec = pl.BlockSpec(memory_space=pl.ANY)
    if first:
        out_shape = (xs_shape, jax.ShapeDtypeStruct((rows,), jnp.int32))
        out_specs = (any_spec, pl.BlockSpec(memory_space=pltpu.SMEM))
    else:
        in_specs.append(any_spec)
        args.append(xs)
        out_shape, out_specs = xs_shape, any_spec
    return pl.pallas_call(
        functools.partial(_dispatch_kernel, tm=tm, n_e=n_e, tile=tile, n_blk=rows // tile, n_assign=n_assign,
                          first=first),
        out_shape=out_shape,
        grid_spec=pltpu.PrefetchScalarGridSpec(
            num_scalar_prefetch=2, grid=(n // tm,),
            in_specs=in_specs,
            out_specs=out_specs,
            scratch_shapes=[pltpu.VMEM((tile,) + tile_shape, F32), pltpu.SemaphoreType.DMA,
                            pltpu.SemaphoreType.DMA]),
        input_output_aliases={} if first else {3: 0},
        compiler_params=_params("arbitrary"),
    )(*args)


def _experts_kernel(blk_e_ref, inv_ref, x_ref, w1_ref, b1_ref, w2_ref, b2_ref, ytok_ref, w1bf, w2bf, obuf, sem,
                    *, d_ff, tile, n_blk):
    i = pl.program_id(0)
    blk = jnp.minimum(i, n_blk - 1)
    e = blk_e_ref[blk]
    prev = blk_e_ref[jnp.maximum(blk - 1, 0)]

    @pl.when((i == 0) | (e != prev))
    def _():
        w1bf[...] = w1_ref[...].astype(BF16)
        w2bf[...] = w2_ref[...].astype(BF16)

    n_chunks = 4
    per_group = tile // (2 * n_chunks)

    def zero_gate(v):
        return jnp.where((v[0, 0] * 0.0) == 7.0, 1, 0).astype(jnp.int32)

    def start_group(grp, gate):
        half = (i + 1) % 2
        for r in range(grp * per_group, (grp + 1) * per_group):
            pltpu.make_async_copy(obuf.at[half, r], ytok_ref.at[inv_ref[(i - 1) * tile + r] + gate], sem).start()

    def compute(scatter):
        x = x_ref[...].reshape(tile, w1bf.shape[0]).astype(BF16)
        ff2 = w1bf.shape[1]
        cw = ff2 // n_chunks
        hs = []
        for c in range(n_chunks):
            hc = jnp.dot(x, w1bf[:, c * cw:(c + 1) * cw], preferred_element_type=F32) + b1_ref[:, c * cw:(c + 1) * cw]
            hs.append(hc)
            if scatter:
                start_group(c, zero_gate(hc))
        h = jnp.concatenate(hs, axis=-1)
        gt = jnp.minimum(h[:, :d_ff], SWIGLU_LIMIT)
        up = jnp.clip(h[:, d_ff:], -SWIGLU_LIMIT, SWIGLU_LIMIT)
        act = ((up + 1.0) * (gt * jax.nn.sigmoid(SWIGLU_ALPHA * gt))).astype(BF16)
        d = w2bf.shape[1]
        dw = d // n_chunks
        ys = []
        for c in range(n_chunks):
            yc = jnp.dot(act, w2bf[:, c * dw:(c + 1) * dw], preferred_element_type=F32) + b2_ref[:, c * dw:(c + 1) * dw]
            ys.append(yc)
            if scatter:
                start_group(n_chunks + c, zero_gate(yc))
        obuf[i % 2] = jnp.concatenate(ys, axis=-1).reshape(obuf.shape[1:])

    def start_scatter():
        for grp in range(2 * n_chunks):
            start_group(grp, 0)

    def wait_scatter():
        pltpu.make_async_copy(obuf.at[0], ytok_ref.at[pl.ds(0, tile)], sem).wait()

    @pl.when(i == 0)
    def _():
        compute(False)

    @pl.when((i > 0) & (i < n_blk))
    def _():
        compute(True)
        wait_scatter()

    @pl.when(i == n_blk)
    def _():
        start_scatter()
        wait_scatter()


def experts(xs, blk_e, inv, w1, b1, w2, b2, layer, tile):
    rows = xs.shape[0]
    tile_shape = xs.shape[1:]
    n_l, n_e, d, ff2 = w1.shape
    d_ff = ff2 // 2
    n_blk = rows // tile
    cur = lambda i, m: jnp.minimum(i, n_blk - 1)
    per_expert = lambda r, c: pl.BlockSpec((None, None, r, c), lambda i, m, v: (layer, m[cur(i, m)], 0, 0))
    return pl.pallas_call(
        functools.partial(_experts_kernel, d_ff=d_ff, tile=tile, n_blk=n_blk),
        out_shape=jax.ShapeDtypeStruct(xs.shape, F32),
        grid_spec=pltpu.PrefetchScalarGridSpec(
            num_scalar_prefetch=2, grid=(n_blk + 1,),
            in_specs=[pl.BlockSpec((tile,) + tile_shape, lambda i, m, v: (cur(i, m), 0, 0)),
                      per_expert(d, ff2), per_expert(1, ff2), per_expert(d_ff, d), per_expert(1, d)],
            out_specs=pl.BlockSpec(memory_space=pl.ANY),
            scratch_shapes=[pltpu.VMEM((d, ff2), BF16), pltpu.VMEM((d_ff, d), BF16),
                            pltpu.VMEM((2, tile) + tile_shape, F32), pltpu.SemaphoreType.DMA]),
        compiler_params=_params("arbitrary"),
    )(blk_e, inv, xs, w1, b1.reshape(n_l, n_e, 1, ff2), w2, b2.reshape(n_l, n_e, 1, d))


def route_tables(counts, counts_first, tile, n_blk):
    n_e = counts.shape[0]
    padded = (counts + tile - 1) // tile * tile
    p_end = jnp.cumsum(padded)
    start = p_end - padded
    blk_start = jnp.arange(n_blk, dtype=jnp.int32) * tile
    blk_e = jnp.minimum(jnp.sum((p_end[None, :] <= blk_start[:, None]).astype(jnp.int32), axis=1), n_e - 1)
    pad_bounds = jnp.concatenate([start + counts_first, start + counts, p_end, p_end[-1:] // tile])
    return start.astype(jnp.int32), blk_e.astype(jnp.int32), pad_bounds.astype(jnp.int32)


def _block_rank_select(gate, valid, block_id, n_blocks, axis):
    gate = jnp.where(valid, gate, NEG_INF)
    cnt = jnp.zeros(gate.shape, F32)
    for n2 in range(n_blocks):
        g2 = gate[n2:n2 + 1, :] if axis == 0 else gate[:, n2:n2 + 1]
        ahead = (g2 > gate) | ((g2 == gate) & (block_id > n2))
        cnt = cnt + jnp.where(ahead, 1.0, 0.0)
    return valid & (cnt < float(MOBA_TOPK))


MOBA_HEADS_PER_STEP = 2


def _moba_prompt_kernel(q_ref, k_ref, v_ref, o_ref, km_ref, kb_ref, vt_ref, sel_ref, acc_ref, s_ref,
                        *, n_blocks, blk, hd, scale):
    i = pl.program_id(2)
    heads = range(MOBA_HEADS_PER_STEP)
    cols = [slice(g * hd, (g + 1) * hd) for g in heads]

    @pl.when(i == 0)
    def _():
        for g in heads:
            for n in range(n_blocks):
                kblk = k_ref[n * blk:(n + 1) * blk, cols[g]]
                km_ref[g, n:n + 1, :] = jnp.mean(kblk, axis=0, keepdims=True)
                kb_ref[g, n * blk:(n + 1) * blk, :] = kblk.astype(BF16)
                vt_ref[g, n] = v_ref[n * blk:(n + 1) * blk, cols[g]].T.astype(BF16)

    start = pl.multiple_of(i * blk, blk)
    key_id = lax.broadcasted_iota(jnp.int32, (blk, LANES), 0)
    q_id = lax.broadcasted_iota(jnp.int32, (blk, LANES), 1)
    blk_id = lax.broadcasted_iota(jnp.int32, (n_blocks, blk), 0)
    chains = [(g, hf) for g in heads for hf in range(blk // LANES)]
    qtb, state = [], []
    for c, (g, hf) in enumerate(chains):
        qcols = slice(hf * LANES, (hf + 1) * LANES)
        if hf == 0:
            qt = q_ref[:, cols[g]].T
            gate = jnp.dot(km_ref[g], qt, precision=HIGHEST, preferred_element_type=F32)
            sel = jnp.where(_block_rank_select(gate, blk_id < i, blk_id, n_blocks, axis=0), 1.0, 0.0)
            qs = (qt * scale).astype(BF16)
        sel_ref[c] = sel[:, qcols]
        qtb.append(qs[:, qcols])
        s = jnp.dot(kb_ref[g, pl.ds(start, blk), :], qtb[c], preferred_element_type=F32)
        s = jnp.where(key_id <= q_id + hf * LANES, s, NEG_INF)
        m = jnp.max(s, axis=0, keepdims=True)
        p = jnp.exp(s - m)
        acc_ref[c] = jnp.dot(vt_ref[g, i], p.astype(BF16), preferred_element_type=F32)
        state += [m, jnp.sum(p, axis=0, keepdims=True)]
        s_ref[c] = jnp.dot(kb_ref[g, 0:blk, :], qtb[c], preferred_element_type=F32)

    def body(n, carry):
        nxt = pl.multiple_of((n + 1) * blk, blk)
        s_cur = [s_ref[c] for c in range(len(chains))]
        s_nxt = [jnp.dot(kb_ref[g, pl.ds(nxt, blk), :], qtb[c], preferred_element_type=F32)
                 for c, (g, hf) in enumerate(chains)]
        out = []
        for c, (g, hf) in enumerate(chains):
            m, l = carry[2 * c:2 * c + 2]
            s = jnp.where(sel_ref[c, pl.ds(n, 1), :] > 0.0, s_cur[c], NEG_INF)
            m_new = jnp.maximum(m, jnp.max(s, axis=0, keepdims=True))
            a = jnp.exp(m - m_new)
            p = jnp.exp(s - m_new)
            acc_ref[c] = a * acc_ref[c] + jnp.dot(vt_ref[g, n], p.astype(BF16), preferred_element_type=F32)
            out += [m_new, a * l + jnp.sum(p, axis=0, keepdims=True)]
        for c in range(len(chains)):
            s_ref[c] = s_nxt[c]
        return tuple(out)

    state = lax.fori_loop(0, i, body, tuple(state))
    for c, (g, hf) in enumerate(chains):
        o_ref[hf * LANES:(hf + 1) * LANES, cols[g]] = (acc_ref[c] / state[2 * c + 1]).T.astype(o_ref.dtype)


def moba_prompt(qkv, n_seq, seq, heads, hd):
    blk = MOBA_BLOCK
    hpg = MOBA_HEADS_PER_STEP
    assert seq % blk == 0 and heads % hpg == 0
    nq = seq // blk
    hg = heads // hpg
    return pl.pallas_call(
        functools.partial(_moba_prompt_kernel, n_blocks=nq, blk=blk, hd=hd, scale=hd ** -0.5),
        out_shape=jax.ShapeDtypeStruct((n_seq * seq, heads * hd), BF16),
        grid=(n_seq, hg, nq),
        in_specs=[pl.BlockSpec((blk, hpg * hd), lambda b, h, i: (b * nq + i, h)),
                  pl.BlockSpec((seq, hpg * hd), lambda b, h, i: (b, hg + h)),
                  pl.BlockSpec((seq, hpg * hd), lambda b, h, i: (b, 2 * hg + h))],
        out_specs=pl.BlockSpec((blk, hpg * hd), lambda b, h, i: (b * nq + i, h)),
        scratch_shapes=[pltpu.VMEM((hpg, nq, hd), F32), pltpu.VMEM((hpg, seq, hd), BF16),
                        pltpu.VMEM((hpg, nq, hd, blk), BF16), pltpu.VMEM((hpg * (blk // LANES), nq, LANES), F32),
                        pltpu.VMEM((hpg * (blk // LANES), hd, LANES), F32),
                        pltpu.VMEM((hpg * (blk // LANES), blk, LANES), F32)],
        compiler_params=_params("arbitrary", "arbitrary", "arbitrary"),
    )(qkv, qkv, qkv)


PAGES_PER_STEP = 8


def _moba_sample_kernel(pt_ref, q_ref, kn_ref, vn_ref, *rest, n_pages, page, heads, n_past_blocks, scale):
    k_refs = rest[:PAGES_PER_STEP]
    v_refs = rest[PAGES_PER_STEP:2 * PAGES_PER_STEP]
    o_ref, sc_ref, gate_ref, acc_ref, l_ref = rest[2 * PAGES_PER_STEP:]
    ph = pl.program_id(1)
    s = pl.program_id(2)
    pages_per_block = MOBA_BLOCK // page
    q = q_ref[...]
    R, hd = q.shape
    cols = page * heads
    q16 = (q * scale).astype(BF16)
    lane = lax.broadcasted_iota(jnp.int32, (R, LANES), 1)

    def head_of(shape, dim):
        return lax.rem(lax.broadcasted_iota(jnp.int32, shape, dim), heads)

    @pl.when((ph == 0) & (s == 0))
    def _():
        gate_ref[...] = jnp.zeros_like(gate_ref)

    @pl.when(ph == 0)
    def _():
        for r in range(PAGES_PER_STEP):
            k3 = k_refs[r][...]
            sc_ref[s * PAGES_PER_STEP + r] = _nt_dot(q16, k3.reshape(cols, hd).astype(BF16))
            ksum = jnp.sum(k3, axis=0) * (1.0 / MOBA_BLOCK)
            g = jnp.sum(q * jnp.concatenate([ksum] * (R // heads), axis=0), axis=-1, keepdims=True)
            blk_id = (s * PAGES_PER_STEP + r) // pages_per_block
            gate_ref[...] += jnp.where(lane == blk_id, g, 0.0)

    @pl.when((ph == 1) & (s == 0))
    def _():
        sel = _block_rank_select(gate_ref[...], lane < n_past_blocks, lane, n_past_blocks, axis=1)
        same_head = head_of((R, cols), 1) == head_of((R, cols), 0)
        s_own = _nt_dot(q16, kn_ref[...].astype(BF16))
        r_i = lax.broadcasted_iota(jnp.int32, (R, R), 0)
        c_i = lax.broadcasted_iota(jnp.int32, (R, R), 1)
        r_h, c_h = head_of((R, R), 0), head_of((R, R), 1)
        s_own = jnp.where((r_h == c_h) & (c_i - c_h <= r_i - r_h), s_own, NEG_INF)
        mx = jnp.full((R, cols), NEG_INF, F32)
        for pg in range(n_pages):
            b_id = pg // pages_per_block
            mx = jnp.maximum(mx, jnp.where(sel[:, b_id:b_id + 1] & same_head, sc_ref[pg], NEG_INF))
        m = jnp.maximum(jnp.max(s_own, axis=-1, keepdims=True), jnp.max(mx, axis=-1, keepdims=True))
        p_own = jnp.exp(s_own - m)
        psum = jnp.zeros((R, cols), F32)
        for pg in range(n_pages):
            b_id = pg // pages_per_block
            p = jnp.where(sel[:, b_id:b_id + 1] & same_head, jnp.exp(sc_ref[pg] - m), 0.0)
            sc_ref[pg] = p
            psum = psum + p
        l = jnp.sum(p_own, axis=-1, keepdims=True) + jnp.sum(psum, axis=-1, keepdims=True)
        l_ref[...] = jnp.broadcast_to(l, l_ref.shape)
        acc_ref[...] = jnp.dot(p_own.astype(BF16), vn_ref[...].astype(BF16), preferred_element_type=F32)

    @pl.when(ph == 1)
    def _():
        acc = acc_ref[...]
        for r in range(PAGES_PER_STEP):
            p = sc_ref[s * PAGES_PER_STEP + r]
            acc = acc + jnp.dot(p.astype(BF16), v_refs[r][...].reshape(cols, hd).astype(BF16),
                                preferred_element_type=F32)
        acc_ref[...] = acc

    @pl.when((ph == 1) & (s == pl.num_programs(2) - 1))
    def _():
        o_ref[...] = acc_ref[...] / l_ref[:, 0:1]


def moba_sample(qkv, cache_k, cache_v, page_table, layer_j, n_seq, t_new, heads, hd):
    d = heads * hd
    n_pool, n_a, page = cache_k.shape[0], cache_k.shape[1], cache_k.shape[2]
    n_pages = page_table.shape[1]
    past = n_pages * page
    assert past % MOBA_BLOCK == 0 and MOBA_BLOCK % page == 0 and t_new <= MOBA_BLOCK
    assert n_pages % PAGES_PER_STEP == 0
    n_steps = n_pages // PAGES_PER_STEP
    n_past_blocks = past // MOBA_BLOCK
    assert MOBA_TOPK <= n_past_blocks <= 128
    R = heads * t_new
    qkv3 = qkv.reshape(n_seq, t_new, 3 * d)
    q, k_new, v_new = (qkv3[:, :, c * d:(c + 1) * d].reshape(n_seq, R, hd) for c in range(3))
    pt = page_table.reshape(-1).astype(jnp.int32)

    def k_map(r):
        return lambda b, ph, s, pt: (pt[b * n_pages + jnp.where(ph == 0, s, n_steps - 1) * PAGES_PER_STEP + r],
                                     layer_j, 0, 0, 0)

    def v_map(r):
        return lambda b, ph, s, pt: (pt[b * n_pages + jnp.where(ph == 0, 0, s) * PAGES_PER_STEP + r],
                                     layer_j, 0, 0, 0)

    seq_blk = pl.BlockSpec((None, R, hd), lambda b, ph, s, pt: (b, 0, 0))
    page_blk = lambda index_map: pl.BlockSpec((None, None, page, heads, hd), index_map)
    o = pl.pallas_call(
        functools.partial(_moba_sample_kernel, n_pages=n_pages, page=page, heads=heads,
                          n_past_blocks=n_past_blocks, scale=hd ** -0.5),
        out_shape=jax.ShapeDtypeStruct((n_seq, R, hd), F32),
        grid_spec=pltpu.PrefetchScalarGridSpec(
            num_scalar_prefetch=1, grid=(n_seq, 2, n_steps),
            in_specs=[seq_blk, seq_blk, seq_blk]
                     + [page_blk(k_map(r)) for r in range(PAGES_PER_STEP)]
                     + [page_blk(v_map(r)) for r in range(PAGES_PER_STEP)],
            out_specs=seq_blk,
            scratch_shapes=[pltpu.VMEM((n_pages, R, page * heads), F32), pltpu.VMEM((R, LANES), F32),
                            pltpu.VMEM((R, hd), F32), pltpu.VMEM((R, LANES), F32)]),
        compiler_params=_params("arbitrary", "arbitrary", "arbitrary"),
    )(pt, q, k_new, v_new, *([cache_k] * PAGES_PER_STEP), *([cache_v] * PAGES_PER_STEP))
    return o.reshape(n_seq * t_new, d)


def _softplus(x):
    return jnp.maximum(x, 0.0) + jnp.log1p(jnp.exp(-jnp.abs(x)))


def _lru_gates(xc, wra_ref, bra, wix_ref, bix, sp, n, bw):
    cs = slice(n * bw, (n + 1) * bw)
    xcb = xc[:, cs]
    xcb16 = xcb.astype(BF16)
    r = jax.nn.sigmoid(jnp.dot(xcb16, wra_ref[n].astype(BF16), preferred_element_type=F32) + bra[:, cs])
    ig = jax.nn.sigmoid(jnp.dot(xcb16, wix_ref[n].astype(BF16), preferred_element_type=F32) + bix[:, cs])
    log_a = -LRU_C * r * sp[:, cs]
    a = jnp.exp(log_a)
    b = jnp.sqrt(jnp.tanh(-log_a) * (1.0 + a * a)) * (ig * xcb)
    return a, b


def _lru_prompt_kernel(yb_ref, xr_ref, cw_ref, cb_ref, wra_ref, bra_ref, wix_ref, bix_ref, lam_ref,
                       z_ref, cs_ref, hl_ref, xbuf, hcar, *, tc, n_lru_blocks, bw, conv_w):
    i = pl.program_id(1)

    @pl.when(i == 0)
    def _():
        xbuf[0:8, :] = jnp.zeros((8, xbuf.shape[1]), F32)
        hcar[...] = jnp.zeros_like(hcar)

    xr = xr_ref[...]
    xbuf[8:8 + tc, :] = xr
    cw = cw_ref[...]
    xc = cb_ref[...] + xr * cw[conv_w - 1:conv_w, :]
    for s in range(1, conv_w):
        xc = xc + xbuf[8 - s:8 - s + tc, :] * cw[conv_w - 1 - s:conv_w - s, :]
    xbuf[0:8, :] = xbuf[tc:tc + 8, :]
    sp = _softplus(-lam_ref[...])
    bra, bix = bra_ref[...], bix_ref[...]
    row = lax.broadcasted_iota(jnp.int32, (tc, bw), 0)
    for n in range(n_lru_blocks):
        cs = slice(n * bw, (n + 1) * bw)
        a, b = _lru_gates(xc, wra_ref, bra, wix_ref, bix, sp, n, bw)
        sh = 1
        while sh < tc:
            keep = row >= sh
            a_prev = jnp.where(keep, pltpu.roll(a, sh, 0), 1.0)
            b_prev = jnp.where(keep, pltpu.roll(b, sh, 0), 0.0)
            b = a * b_prev + b
            a = a * a_prev
            sh *= 2
        h = a * hcar[:, cs] + b
        hcar[:, cs] = h[tc - 1:tc, :]
        z_ref[:, cs] = (h * jax.nn.gelu(yb_ref[:, cs])).astype(z_ref.dtype)
    cs_ref[...] = xbuf[8 - (conv_w - 1):8, :]
    hl_ref[...] = hcar[...]


def lru_prompt(br, n_seq, seq, conv_w, conv_b, w_ra, b_ra, w_ix, b_ix, lam, tc=256):
    w = br.shape[1] // 2
    cw = conv_w.shape[0]
    nb, bw = w_ra.shape[0], w_ra.shape[1]
    assert seq % tc == 0 and cw - 1 <= 8 <= tc
    nt = seq // tc
    vec = pl.BlockSpec((1, w), lambda b, i: (0, 0))
    wblk = pl.BlockSpec((nb, bw, bw), lambda b, i: (0, 0, 0))
    return pl.pallas_call(
        functools.partial(_lru_prompt_kernel, tc=tc, n_lru_blocks=nb, bw=bw, conv_w=cw),
        out_shape=(jax.ShapeDtypeStruct((n_seq * seq, w), BF16),
                   jax.ShapeDtypeStruct((n_seq, cw - 1, w), F32),
                   jax.ShapeDtypeStruct((n_seq, 1, w), F32)),
        grid=(n_seq, nt),
        in_specs=[pl.BlockSpec((tc, w), lambda b, i: (b * nt + i, 0)),
                  pl.BlockSpec((tc, w), lambda b, i: (b * nt + i, 1)),
                  pl.BlockSpec((cw, w), lambda b, i: (0, 0)), vec, wblk, vec, wblk, vec, vec],
        out_specs=(pl.BlockSpec((tc, w), lambda b, i: (b * nt + i, 0)),
                   pl.BlockSpec((None, cw - 1, w), lambda b, i: (b, 0, 0)),
                   pl.BlockSpec((None, 1, w), lambda b, i: (b, 0, 0))),
        scratch_shapes=[pltpu.VMEM((8 + tc, w), F32), pltpu.VMEM((1, w), F32)],
        compiler_params=_params("arbitrary", "arbitrary"),
    )(br, br, conv_w, conv_b.reshape(1, w), w_ra, b_ra.reshape(1, w), w_ix, b_ix.reshape(1, w), lam.reshape(1, w))


def _lru_sample_kernel(br_ref, c0_ref, h0_ref, cw_ref, cb_ref, wra_ref, bra_ref, wix_ref, bix_ref, lam_ref,
                       z_ref, cs_ref, hl_ref, *, t_new, n_lru_blocks, bw, conv_w, w):
    br = br_ref[...]
    yb, xr = br[:, :w], br[:, w:]
    xpad = jnp.concatenate([c0_ref[...], xr], axis=0)
    cw = cw_ref[...]
    xc = cb_ref[...] + xpad[0:t_new, :] * cw[0:1, :]
    for i in range(1, conv_w):
        xc = xc + xpad[i:i + t_new, :] * cw[i:i + 1, :]
    sp = _softplus(-lam_ref[...])
    bra, bix = bra_ref[...], bix_ref[...]
    gelu_y = jax.nn.gelu(yb)
    for n in range(n_lru_blocks):
        cs = slice(n * bw, (n + 1) * bw)
        a, b = _lru_gates(xc, wra_ref, bra, wix_ref, bix, sp, n, bw)
        h = h0_ref[:, cs]
        for t in range(t_new):
            h = a[t:t + 1, :] * h + b[t:t + 1, :]
            z_ref[t:t + 1, cs] = h * gelu_y[t:t + 1, cs]
        hl_ref[:, cs] = h
    cs_ref[...] = xpad[t_new:t_new + conv_w - 1, :]


def lru_sample(br, conv0, h0, n_seq, t_new, conv_w, conv_b, w_ra, b_ra, w_ix, b_ix, lam):
    w = br.shape[1] // 2
    cw = conv_w.shape[0]
    nb, bw = w_ra.shape[0], w_ra.shape[1]
    vec = pl.BlockSpec((1, w), lambda b: (0, 0))
    wblk = pl.BlockSpec((nb, bw, bw), lambda b: (0, 0, 0))
    per_seq = lambda rows, cols: pl.BlockSpec((None, rows, cols), lambda b: (b, 0, 0))
    z, cs, hl = pl.pallas_call(
        functools.partial(_lru_sample_kernel, t_new=t_new, n_lru_blocks=nb, bw=bw, conv_w=cw, w=w),
        out_shape=(jax.ShapeDtypeStruct((n_seq, t_new, w), F32),
                   jax.ShapeDtypeStruct((n_seq, cw - 1, w), F32),
                   jax.ShapeDtypeStruct((n_seq, 1, w), F32)),
        grid=(n_seq,),
        in_specs=[per_seq(t_new, 2 * w), per_seq(cw - 1, w), per_seq(1, w),
                  pl.BlockSpec((cw, w), lambda b: (0, 0)), vec, wblk, vec, wblk, vec, vec],
        out_specs=(per_seq(t_new, w), per_seq(cw - 1, w), per_seq(1, w)),
        compiler_params=_params("arbitrary"),
    )(br.reshape(n_seq, t_new, 2 * w), conv0, h0.reshape(n_seq, 1, w), conv_w, conv_b.reshape(1, w),
      w_ra, b_ra.reshape(1, w), w_ix, b_ix.reshape(1, w), lam.reshape(1, w))
    return z.reshape(n_seq * t_new, w), cs, hl


def _rope(x, cos, sin, half):
    x1, x2 = x[:, :half], x[:, half:]
    return jnp.concatenate([x1 * cos - x2 * sin, x2 * cos + x1 * sin], axis=-1)


def _group_norm_gate(o, g, gn_g, gn_b):
    mu = jnp.mean(o, axis=-1, keepdims=True)
    d = o - mu
    var = jnp.mean(d * d, axis=-1, keepdims=True)
    on = d * lax.rsqrt(var + LN_EPS) * gn_g + gn_b
    return (g * jax.nn.sigmoid(g)) * on


def _ret_prompt_kernel(q_ref, k_ref, v_ref, g_ref, cos_ref, sin_ref, dm_ref, cd_ref, kd_ref, chd_ref,
                       gng_ref, gnb_ref, z_ref, s_ref, S, *, kscale, half):
    c = pl.program_id(2)

    @pl.when(c == 0)
    def _():
        S[...] = jnp.zeros_like(S)

    cos, sin = cos_ref[...], sin_ref[...]
    q = _rope(q_ref[...], cos, sin, half)
    k = _rope(k_ref[...] * kscale, cos, sin, half)
    qb, kb, vb = q.astype(BF16), k.astype(BF16), v_ref[...].astype(BF16)
    inner = _nt_dot(qb, kb) * dm_ref[...]
    s_old = S[...]
    o = (jnp.dot(inner.astype(BF16), vb, preferred_element_type=F32)
         + jnp.dot(qb, s_old.astype(BF16), preferred_element_type=F32) * cd_ref[...])
    kdt = (k * kd_ref[...]).T.astype(BF16)
    S[...] = chd_ref[...] * s_old + jnp.dot(kdt, vb, preferred_element_type=F32)
    z_ref[...] = _group_norm_gate(o, g_ref[...], gng_ref[...], gnb_ref[...]).astype(z_ref.dtype)

    @pl.when(c == pl.num_programs(2) - 1)
    def _():
        s_ref[...] = S[...]


def _decay_tables(heads, c):
    f = np.float32
    lg = np.log1p(-np.exp2(f(-5.0) - np.arange(heads, dtype=f))).astype(f)
    idx = np.arange(c, dtype=f)
    diff = idx[:, None] - idx[None, :]
    dmask = np.where(diff >= 0, np.exp(np.maximum(diff, f(0.0))[None] * lg[:, None, None]), f(0.0)).astype(f)
    cross = np.exp((idx + f(1.0))[None, :] * lg[:, None])[..., None].astype(f)
    kdec = np.exp((f(c) - f(1.0) - idx)[None, :] * lg[:, None])[..., None].astype(f)
    chunk = np.exp(f(c) * lg)[:, None, None].astype(f)
    return dmask, cross, kdec, chunk


def _rope_tables(pos0, t, half):
    f = np.float32
    inv = (f(ROPE_BASE) ** (-np.arange(half, dtype=f) / f(half))).astype(f)
    ang = (np.arange(pos0, pos0 + t).astype(f)[:, None] * inv[None, :]).astype(f)
    return np.cos(ang).astype(f), np.sin(ang).astype(f)


def ret_prompt(proj, n_seq, seq, heads, kdim, vdim, gn_g, gn_b, tc=256):
    assert seq % tc == 0 and vdim % kdim == 0
    nc = seq // tc
    hk, hv = heads * kdim, heads * vdim
    half = kdim // 2
    cos, sin = _rope_tables(0, seq, half)
    dmask, cross, kdec, chunk = _decay_tables(heads, tc)
    v0, g0 = 2 * hk // vdim, (2 * hk + hv) // vdim
    row = lambda b, h, c: b * nc + c
    per_head = lambda r, cdim: pl.BlockSpec((None, r, cdim), lambda b, h, c: (h, 0, 0))
    return pl.pallas_call(
        functools.partial(_ret_prompt_kernel, kscale=kdim ** -0.5, half=half),
        out_shape=(jax.ShapeDtypeStruct((n_seq * seq, hv), BF16),
                   jax.ShapeDtypeStruct((n_seq, heads, kdim, vdim), F32)),
        grid=(n_seq, heads, nc),
        in_specs=[pl.BlockSpec((tc, kdim), lambda b, h, c: (row(b, h, c), h)),
                  pl.BlockSpec((tc, kdim), lambda b, h, c: (row(b, h, c), heads + h)),
                  pl.BlockSpec((tc, vdim), lambda b, h, c: (row(b, h, c), v0 + h)),
                  pl.BlockSpec((tc, vdim), lambda b, h, c: (row(b, h, c), g0 + h)),
                  pl.BlockSpec((tc, half), lambda b, h, c: (c, 0)),
                  pl.BlockSpec((tc, half), lambda b, h, c: (c, 0)),
                  per_head(tc, tc), per_head(tc, 1), per_head(tc, 1), per_head(1, 1),
                  pl.BlockSpec((1, vdim), lambda b, h, c: (0, h)),
                  pl.BlockSpec((1, vdim), lambda b, h, c: (0, h))],
        out_specs=(pl.BlockSpec((tc, vdim), lambda b, h, c: (row(b, h, c), h)),
                   pl.BlockSpec((None, None, kdim, vdim), lambda b, h, c: (b, h, 0, 0))),
        scratch_shapes=[pltpu.VMEM((kdim, vdim), F32)],
        compiler_params=_params("arbitrary", "arbitrary", "arbitrary"),
    )(proj, proj, proj, proj, cos, sin, dmask, cross, kdec, chunk, gn_g.reshape(1, hv), gn_b.reshape(1, hv))


def _ret_sample_kernel(p_ref, kt_ref, s0_ref, cos_ref, sin_ref, cost_ref, sint_ref, dm_ref, cd_ref, kd_ref, chd_ref,
                       gng_ref, gnb_ref, z_ref, s_ref, *, heads, kdim, vdim, t_new, kscale):
    half = kdim // 2
    hk, hv = heads * kdim, heads * vdim
    p = p_ref[...]
    cos, sin = cos_ref[...], sin_ref[...]
    cost, sint = cost_ref[...], sint_ref[...]
    for h in range(heads):
        q = _rope(p[:, h * kdim:(h + 1) * kdim], cos, sin, half)
        k = _rope(p[:, hk + h * kdim:hk + (h + 1) * kdim] * kscale, cos, sin, half)
        v = p[:, 2 * hk + h * vdim:2 * hk + (h + 1) * vdim]
        g = p[:, 2 * hk + hv + h * vdim:2 * hk + hv + (h + 1) * vdim]
        kt = kt_ref[h] * kscale
        k1, k2 = kt[:half, :], kt[half:, :]
        kt = jnp.concatenate([k1 * cost - k2 * sint, k2 * cost + k1 * sint], axis=0) * kd_ref[h]
        s0 = s0_ref[h]
        dm = dm_ref[h]
        o = jnp.dot(q, s0, preferred_element_type=F32, precision=HIGHEST) * cd_ref[h]
        s_new = chd_ref[h] * s0
        for m in range(t_new):
            inner_m = jnp.sum(q * k[m:m + 1, :], axis=-1, keepdims=True) * dm[:, m:m + 1]
            o = o + inner_m * v[m:m + 1, :]
            s_new = s_new + kt[:, m:m + 1] * v[m:m + 1, :]
        s_ref[h] = s_new
        z_ref[:, h * vdim:(h + 1) * vdim] = _group_norm_gate(
            o, g, gng_ref[:, h * vdim:(h + 1) * vdim], gnb_ref[:, h * vdim:(h + 1) * vdim])


def ret_sample(proj, s0, pos0, n_seq, t_new, heads, kdim, vdim, gn_g, gn_b):
    hk, hv = heads * kdim, heads * vdim
    half = kdim // 2
    width = 2 * hk + 2 * hv
    cos, sin = _rope_tables(pos0, t_new, half)
    dmask, cross, kdec, chunk = _decay_tables(heads, t_new)
    proj3 = proj.reshape(n_seq, t_new, width)
    kt = proj3[:, :, hk:2 * hk].reshape(n_seq, t_new, heads, kdim).transpose(0, 2, 3, 1)
    full = lambda *shape: pl.BlockSpec(shape, lambda b: (0,) * len(shape))
    z, s = pl.pallas_call(
        functools.partial(_ret_sample_kernel, heads=heads, kdim=kdim, vdim=vdim, t_new=t_new, kscale=kdim ** -0.5),
        out_shape=(jax.ShapeDtypeStruct((n_seq, t_new, hv), F32),
                   jax.ShapeDtypeStruct((n_seq, heads, kdim, vdim), F32)),
        grid=(n_seq,),
        in_specs=[pl.BlockSpec((None, t_new, width), lambda b: (b, 0, 0)),
                  pl.BlockSpec((None, heads, kdim, t_new), lambda b: (b, 0, 0, 0)),
                  pl.BlockSpec((None, heads, kdim, vdim), lambda b: (b, 0, 0, 0)),
                  full(t_new, half), full(t_new, half), full(half, t_new), full(half, t_new),
                  full(heads, t_new, t_new), full(heads, t_new, 1), full(heads, 1, t_new), full(heads, 1, 1),
                  full(1, hv), full(1, hv)],
        out_specs=(pl.BlockSpec((None, t_new, hv), lambda b: (b, 0, 0)),
                   pl.BlockSpec((None, heads, kdim, vdim), lambda b: (b, 0, 0, 0))),
        compiler_params=_params("arbitrary"),
    )(proj3, kt, s0, cos, sin, cos.T, sin.T, dmask, cross, kdec.transpose(0, 2, 1), chunk,
      gn_g.reshape(1, hv), gn_b.reshape(1, hv))
    return z.reshape(n_seq * t_new, hv), s


def kernel(x_prompt, x_sample, cache_k, cache_v, state_lru_conv, state_lru_h, state_ret, page_table, c_prompt, c_sample, w_ada, b_ada, ln_g, ln_b, a_w_qkv, a_w_o, lru_w_in, lru_b_in, lru_conv_w, lru_conv_b, lru_w_ra, lru_b_ra, lru_w_ix, lru_b_ix, lru_lambda, lru_w_out, ret_w_in, ret_gn_g, ret_gn_b, ret_w_o, moe_w_router, moe_b_router, moe_w1, moe_b1, moe_w2, moe_b2):
    depth, d = w_ada.shape[0], w_ada.shape[1]
    n_mixers = 3
    alpha = (2 * depth) ** 0.25
    heads, hd = cache_k.shape[3], cache_k.shape[4]
    r_heads, r_kdim, r_vdim = state_ret.shape[2], state_ret.shape[3], state_ret.shape[4]
    n_e = moe_w_router.shape[2]
    past_len = page_table.shape[1] * cache_k.shape[2]

    bp, bs = c_prompt.shape[0], c_sample.shape[0]
    n_c = bp + bs
    c_rows = 16 * (-(-n_c // 16))
    c_all = jnp.pad(jnp.concatenate([c_prompt, c_sample], axis=0), ((0, c_rows - n_c), (0, 0)))
    mods = [matmul(c_all, w_ada, b_ada, layer=l, pre_silu=True) for l in range(depth)]

    class Group:
        def __init__(self, x, seq_lo, sample):
            self.sample = sample
            self.n_seq, self.t, _ = x.shape
            self.n = self.n_seq * self.t
            self.tm = min(ROW_TILE, self.n)
            self.ms = [ModSource(mods[l][seq_lo:seq_lo + self.n_seq], self.t, self.tm, d) for l in range(depth)]
            self.x2 = x.reshape(self.n, d)
            self.u = modulate(self.x2, self.ms[0], 1, 0)
            self.ks, self.vs, self.convs, self.hs, self.ss = [], [], [], [], []
            self.k_all = self.v_all = None

        def outputs(self):
            if self.k_all is None:
                lead = (self.n_seq, self.t, heads, hd)
                self.k_all = jnp.stack([k.reshape(lead) for k in self.ks], 1)
                self.v_all = jnp.stack([v.reshape(lead) for v in self.vs], 1)
            return (self.x2.reshape(self.n_seq, self.t, d), self.k_all, self.v_all,
                    jnp.stack(self.convs, 1), jnp.stack(self.hs, 1), jnp.stack(self.ss, 1))

    def mixer(g, layer):
        kind, j = layer % n_mixers, layer // n_mixers
        u, sample, n_seq, t = g.u, g.sample, g.n_seq, g.t
        if kind == 0:
            if sample:
                qkv = matmul(u, a_w_qkv, layer=j)
                o = moba_sample(qkv, cache_k, cache_v, page_table, j, n_seq, t, heads, hd)
                k3, v3 = qkv[:, d:2 * d], qkv[:, 2 * d:]
                g.ks.append(k3.reshape(n_seq, t, heads, hd))
                g.vs.append(v3.reshape(n_seq, t, heads, hd))
            elif 0 < j == a_w_qkv.shape[0] - 1:
                qkv, g.k_all, g.v_all = qkv_project(u, a_w_qkv[j], heads, hd, n_seq, t, g.ks, g.vs)
                o = moba_prompt(qkv, n_seq, t, heads, hd)
            else:
                qkv, k3, v3 = qkv_project(u, a_w_qkv[j], heads, hd, n_seq, t)
                o = moba_prompt(qkv, n_seq, t, heads, hd)
                g.ks.append(k3)
                g.vs.append(v3)
            return matmul(o, a_w_o, layer=j)
        if kind == 1:
            br = matmul(u, lru_w_in[j], lru_b_in[j])
            lru_args = (lru_conv_w[j], lru_conv_b[j], lru_w_ra[j], lru_b_ra[j], lru_w_ix[j], lru_b_ix[j],
                        lru_lambda[j])
            if sample:
                z, cs, hl = lru_sample(br, state_lru_conv[:, j], state_lru_h[:, j], n_seq, t, *lru_args)
            else:
                z, cs, hl = lru_prompt(br, n_seq, t, *lru_args)
            g.convs.append(cs)
            g.hs.append(hl.reshape(n_seq, -1))
            return matmul(z, lru_w_out[j])
        proj = matmul(u, ret_w_in[j])
        if sample:
            z, s_fin = ret_sample(proj, state_ret[:, j], past_len, n_seq, t, r_heads, r_kdim, r_vdim,
                                  ret_gn_g[j], ret_gn_b[j])
        else:
            z, s_fin = ret_prompt(proj, n_seq, t, r_heads, r_kdim, r_vdim, ret_gn_g[j], ret_gn_b[j])
        g.ss.append(s_fin)
        return matmul(z, ret_w_o[j])

    groups = [Group(x_prompt, 0, False), Group(x_sample, bp, True)]
    tile = MOE_TILE
    n_assign = sum(g.n for g in groups) * MOE_TOPK
    n_blk = -(-n_assign // tile) + n_e
    for layer in range(depth):
        counts = [jnp.zeros((1, n_e), jnp.int32)]
        routed = []
        for g in groups:
            y = mixer(g, layer)
            g.x2, u3, top_e, gates, rank, cnt = ln_router(
                g.x2, y, counts[-1], g.ms[layer], 2, 4, 3, ln_g[layer, 0], ln_b[layer, 0],
                moe_w_router[layer], moe_b_router[layer], alpha)
            counts.append(cnt)
            routed.append((u3, top_e, gates, rank))
        start, blk_e, pad_bounds = route_tables(counts[-1].reshape(n_e), counts[1].reshape(n_e), tile, n_blk)
        xs, inv, first_assign = None, None, []
        for g, (u3, top_e, gates, rank) in zip(groups, routed):
            e_ids = jnp.arange(n_e, dtype=jnp.int32)
            first_row = jnp.sum(jnp.where(top_e[..., None] == e_ids, start, 0), axis=-1)
            pos = (first_row + rank).reshape(g.n * MOE_TOPK)
            first_assign.append(sum(h.n for h in groups[:len(first_assign)]) * MOE_TOPK)
            if xs is None:
                xs, inv = dispatch(u3, pos, pad_bounds, None, n_blk * tile, g.tm, n_e, tile, n_assign)
            else:
                xs = dispatch(u3, pos, pad_bounds, xs, n_blk * tile, g.tm, n_e, tile, n_assign)
                inv = inv.at[pos].set(first_assign[-1] + jnp.arange(g.n * MOE_TOPK, dtype=jnp.int32))
        ytok = experts(xs, blk_e, inv, moe_w1, moe_b1, moe_w2, moe_b2, layer, tile)
        for g, (u3, top_e, gates, rank), a0 in zip(groups, routed, first_assign):
            nxt = (g.ms[layer + 1], 1, 0) if layer + 1 < depth else ()
            g.x2, g.u = ln_combine(g.x2, gates, ytok, a0, g.ms[layer], 5, ln_g[layer, 1], ln_b[layer, 1], alpha,
                                   *nxt)

    y_p, k_p, v_p, conv_p, h_p, s_p = groups[0].outputs()
    y_s, k_s, v_s, conv_s, h_s, s_s = groups[1].outputs()
    return (y_p, y_s, k_p, v_p, conv_p, h_p, s_p, k_s, v_s, conv_s, h_s, s_s)
```
